```python
import jax, jax.numpy as jnp
from jax import lax
import numpy as np

D_MODEL = 1024
BATCH = 32
SEQ = 2048
DEPTH = 2

ROPE_THETA = 10000.0
NORM_EPS = 1e-6
ATTN_BLOCK = 128
NEG_INF = -1e30

MLA_HEADS = 8
MLA_NOPE_DIM = 64
MLA_ROPE_DIM = 32
MLA_QK_DIM = MLA_NOPE_DIM + MLA_ROPE_DIM
MLA_V_DIM = 64
MLA_KV_RANK = 256
MLA_Q_RANK = 384

RET_HEADS = 8
RET_QK_DIM = 64
RET_V_DIM = 64
RET_CHUNK = 128

SWA_HEADS = 16
SWA_KV_HEADS = 2
SWA_HEAD_DIM = 64
SWA_WINDOW = 128

N_EXPERTS = 32
TOP_K = 4
D_EXPERT = 1024
SWIGLU_LIMIT = 7.0
SWIGLU_ALPHA = 1.702
MOE_BLOCK = 512

N_EVEN = (DEPTH + 1) // 2
N_ODD = DEPTH // 2

HYB_SPLITS = [MLA_Q_RANK, MLA_KV_RANK, MLA_ROPE_DIM,
              RET_HEADS * RET_QK_DIM, RET_HEADS * RET_QK_DIM,
              RET_HEADS * RET_V_DIM, RET_HEADS * RET_V_DIM]
HYB_IN = sum(HYB_SPLITS)
HYB_MIX = MLA_HEADS * MLA_V_DIM + RET_HEADS * RET_V_DIM
SWA_QKV = (SWA_HEADS + 2 * SWA_KV_HEADS) * SWA_HEAD_DIM

kernel_name = "hybrid_mla_retention_swa_moe_adaln"


def rms_norm(x, g):
    xf = x.astype(jnp.float32)
    y = xf * lax.rsqrt(jnp.mean(xf * xf, axis=-1, keepdims=True) + NORM_EPS)
    return (y * g).astype(x.dtype)


def rope(x, positions):
    half = x.shape[-1] // 2
    inv_freq = ROPE_THETA ** (-jnp.arange(half, dtype=jnp.float32) / half)
    ang = positions.astype(jnp.float32)[:, :, None] * inv_freq
    cos = jnp.cos(ang)[:, :, None, :]
    sin = jnp.sin(ang)[:, :, None, :]
    xf = x.astype(jnp.float32)
    x1, x2 = xf[..., :half], xf[..., half:]
    return jnp.concatenate([x1 * cos - x2 * sin, x2 * cos + x1 * sin], axis=-1).astype(x.dtype)


def causal_block_attention(q, k, v):
    B, S, H, dq = q.shape
    nb = S // ATTN_BLOCK
    scale = dq ** -0.5
    qb = q.reshape(B, nb, ATTN_BLOCK, H, dq).transpose(1, 0, 2, 3, 4)
    k_idx = jnp.arange(S)

    def one_block(args):
        q_blk, n = args
        s = jnp.einsum('bqhd,bkhd->bhqk', q_blk, k).astype(jnp.float32) * scale
        q_idx = n * ATTN_BLOCK + jnp.arange(ATTN_BLOCK)
        s = jnp.where(k_idx[None, :] <= q_idx[:, None], s, NEG_INF)
        p = jax.nn.softmax(s, axis=-1).astype(v.dtype)
        return jnp.einsum('bhqk,bkhd->bqhd', p, v)

    out = lax.map(one_block, (qb, jnp.arange(nb)))
    return out.transpose(1, 0, 2, 3, 4).reshape(B, S, H, v.shape[-1])


def retention_chunkwise(q, k, v, log_gamma):
    B, S, H, dk = q.shape
    dv = v.shape[-1]
    C = RET_CHUNK
    nc = S // C
    q = q.reshape(B, nc, C, H, dk)
    k = k.reshape(B, nc, C, H, dk)
    v = v.reshape(B, nc, C, H, dv)
    idx = jnp.arange(C, dtype=jnp.float32)
    diff = idx[:, None] - idx[None, :]
    decay_intra = jnp.where(diff >= 0, jnp.exp(log_gamma[:, None, None] * jnp.maximum(diff, 0.0)), 0.0)
    scores = jnp.einsum('bnqhd,bnkhd->bnhqk', q, k) * decay_intra
    inner = jnp.einsum('bnhqk,bnkhe->bnqhe', scores, v)
    k_decay = jnp.exp(log_gamma[None, :] * (C - 1 - idx)[:, None])
    kv = jnp.einsum('bnkhd,bnkhe->bnhde', k * k_decay[:, :, None], v)
    chunk_decay = jnp.exp(log_gamma * C).astype(kv.dtype)[:, None, None]

    def step(state, kv_n):
        return state * chunk_decay + kv_n, state

    _, state_before = lax.scan(step, jnp.zeros((B, H, dk, dv), kv.dtype), kv.transpose(1, 0, 2, 3, 4))
    q_decay = jnp.exp(log_gamma[None, :] * (idx + 1.0)[:, None])
    cross = jnp.einsum('bnqhd,nbhde->bnqhe', q * q_decay[:, :, None], state_before)
    return (inner + cross).reshape(B, S, H, dv)


def head_group_norm(y, g):
    yf = y.astype(jnp.float32)
    mu = jnp.mean(yf, axis=-1, keepdims=True)
    var = jnp.mean((yf - mu) ** 2, axis=-1, keepdims=True)
    return (yf - mu) * lax.rsqrt(var + NORM_EPS) * g


def mla_retention_mixer(h, positions, w_in, cq_norm_g, ckv_norm_g, w_uq, w_ukv,
                        q_head_g, k_head_g, ret_norm_g, w_out):
    B, S, _ = h.shape
    proj = h @ w_in
    cuts = list(np.cumsum(HYB_SPLITS)[:-1])
    cq, ckv, k_rope_in, rq, rk, rv, rg = jnp.split(proj, cuts, axis=-1)
    q = (rms_norm(cq, cq_norm_g) @ w_uq).reshape(B, S, MLA_HEADS, MLA_QK_DIM)
    q_nope = rms_norm(q[..., :MLA_NOPE_DIM], q_head_g[:MLA_NOPE_DIM])
    q_rope = rope(rms_norm(q[..., MLA_NOPE_DIM:], q_head_g[MLA_NOPE_DIM:]), positions)
    kv = (rms_norm(ckv, ckv_norm_g) @ w_ukv).reshape(B, S, MLA_HEADS, MLA_NOPE_DIM + MLA_V_DIM)
    k_nope = rms_norm(kv[..., :MLA_NOPE_DIM], k_head_g[:MLA_NOPE_DIM])
    v = kv[..., MLA_NOPE_DIM:]
    k_rope = rope(rms_norm(k_rope_in, k_head_g[MLA_NOPE_DIM:])[:, :, None, :], positions)
    q_full = jnp.concatenate([q_nope, q_rope], axis=-1)
    k_full = jnp.concatenate([k_nope, jnp.broadcast_to(k_rope, (B, S, MLA_HEADS, MLA_ROPE_DIM))], axis=-1)
    attn = causal_block_attention(q_full, k_full, v).reshape(B, S, MLA_HEADS * MLA_V_DIM)
    log_gamma = jnp.log1p(-jnp.exp2(-5.0 - jnp.arange(RET_HEADS, dtype=jnp.float32)))
    rq = rope(rq.reshape(B, S, RET_HEADS, RET_QK_DIM), positions)
    rk = rope(rk.reshape(B, S, RET_HEADS, RET_QK_DIM), positions) * (RET_QK_DIM ** -0.5)
    rv = rv.reshape(B, S, RET_HEADS, RET_V_DIM)
    y = retention_chunkwise(rq, rk, rv, log_gamma)
    y = head_group_norm(y, ret_norm_g).reshape(B, S, RET_HEADS * RET_V_DIM)
    y = (y * jax.nn.silu(rg.astype(jnp.float32))).astype(h.dtype)
    return jnp.concatenate([attn, y], axis=-1) @ w_out


def sliding_window_attention(q, k, v, sinks):
    B, S, Hq, d = q.shape
    Hkv = k.shape[2]
    R = Hq // Hkv
    nb = S // SWA_WINDOW
    scale = d ** -0.5

    def band(t):
        tb = t.reshape(B, nb, SWA_WINDOW, Hkv, t.shape[-1])
        prev = jnp.concatenate([jnp.zeros_like(tb[:, :1]), tb[:, :-1]], axis=1)
        return jnp.concatenate([prev, tb], axis=2).transpose(1, 0, 2, 3, 4)

    kb, vb = band(k), band(v)
    qb = q.reshape(B, nb, SWA_WINDOW, Hkv, R, d).transpose(1, 0, 2, 3, 4, 5)
    i = jnp.arange(SWA_WINDOW)[:, None]
    j = jnp.arange(2 * SWA_WINDOW)[None, :]
    rel = i + SWA_WINDOW - j
    local_mask = (rel >= 0) & (rel < SWA_WINDOW)
    sink = sinks.astype(jnp.float32).reshape(Hkv, R)

    def one_block(args):
        q_blk, k_blk, v_blk, n = args
        s = jnp.einsum('bqgrd,bkgd->bgrqk', q_blk, k_blk).astype(jnp.float32) * scale
        mask = local_mask & ((n - 1) * SWA_WINDOW + j >= 0)
        s = jnp.where(mask, s, NEG_INF)
        sink_col = jnp.broadcast_to(sink[None, :, :, None, None], s.shape[:-1] + (1,))
        p = jax.nn.softmax(jnp.concatenate([s, sink_col], axis=-1), axis=-1)[..., :-1].astype(v_blk.dtype)
        return jnp.einsum('bgrqk,bkgd->bqgrd', p, v_blk)

    out = lax.map(one_block, (qb, kb, vb, jnp.arange(nb)))
    return out.transpose(1, 0, 2, 3, 4, 5).reshape(B, S, Hq, d)


def swa_mixer(h, positions, w_qkv, b_qkv, q_head_g, k_head_g, sinks, w_out, b_out):
    B, S, _ = h.shape
    qkv = h @ w_qkv + b_qkv
    nq = SWA_HEADS * SWA_HEAD_DIM
    nk = SWA_KV_HEADS * SWA_HEAD_DIM
    q, k, v = jnp.split(qkv, [nq, nq + nk], axis=-1)
    q = rope(rms_norm(q.reshape(B, S, SWA_HEADS, SWA_HEAD_DIM), q_head_g), positions)
    k = rope(rms_norm(k.reshape(B, S, SWA_KV_HEADS, SWA_HEAD_DIM), k_head_g), positions)
    v = v.reshape(B, S, SWA_KV_HEADS, SWA_HEAD_DIM)
    o = sliding_window_attention(q, k, v, sinks).reshape(B, S, nq)
    return o @ w_out + b_out


def moe_ffn(h, router_w, router_b, w_gu, b_gu, w_down, b_down):
    B, S, D = h.shape
    xf = h.reshape(-1, D)
    N = xf.shape[0]
    logits = (xf @ router_w + router_b).astype(jnp.float32)
    top_v, top_i = lax.top_k(logits, TOP_K)
    gates = jax.nn.softmax(top_v, axis=-1)
    M = N * TOP_K
    flat_e = top_i.reshape(-1)
    flat_tok = jnp.arange(M, dtype=jnp.int32) // TOP_K
    flat_g = gates.reshape(-1)
    order = jnp.argsort(flat_e)
    sorted_e = flat_e[order]
    counts = jnp.bincount(flat_e, length=N_EXPERTS)
    padded = ((counts + MOE_BLOCK - 1) // MOE_BLOCK) * MOE_BLOCK
    start = jnp.cumsum(counts) - counts
    pad_end = jnp.cumsum(padded)
    pad_start = pad_end - padded
    dest = pad_start[sorted_e] + (jnp.arange(M) - start[sorted_e])
    n_blocks = -(-M // MOE_BLOCK) + N_EXPERTS
    P = n_blocks * MOE_BLOCK
    slot_tok = jnp.full((P,), N, jnp.int32).at[dest].set(flat_tok[order])
    slot_gate = jnp.zeros((P,), jnp.float32).at[dest].set(flat_g[order])
    block_e = jnp.minimum(jnp.searchsorted(pad_end, jnp.arange(n_blocks) * MOE_BLOCK, side='right'),
                          N_EXPERTS - 1)
    x_pad = jnp.concatenate([xf, jnp.zeros((1, D), xf.dtype)], axis=0)

    def body(y, blk):
        tok, g, e = blk
        gu = x_pad[tok] @ w_gu[e] + b_gu[e]
        glu, lin = gu[:, :D_EXPERT], gu[:, D_EXPERT:]
        glu = jnp.minimum(glu, SWIGLU_LIMIT)
        lin = jnp.clip(lin, -SWIGLU_LIMIT, SWIGLU_LIMIT)
        act = glu * jax.nn.sigmoid(SWIGLU_ALPHA * glu) * (lin + 1.0)
        out = (act @ w_down[e] + b_down[e]) * g[:, None]
        return y.at[tok].add(out.astype(y.dtype)), None

    y, _ = lax.scan(body, jnp.zeros((N + 1, D), h.dtype),
                    (slot_tok.reshape(n_blocks, MOE_BLOCK), slot_gate.reshape(n_blocks, MOE_BLOCK), block_e))
    return y[:N].reshape(B, S, D)


def setup_inputs(seed: int = 0) -> dict:
    key = jax.random.key(seed)
    ks = jax.random.split(key, 32)
    f32 = jnp.float32

    def nrm(k, shape, scale):
        return jax.random.normal(k, shape, f32) * scale

    def gain(k, shape):
        return 1.0 + 0.02 * jax.random.normal(k, shape, f32)

    D = D_MODEL
    offsets = jax.random.randint(ks[2], (BATCH, 1), 0, SEQ, dtype=jnp.int32)
    positions = offsets + jnp.arange(SEQ, dtype=jnp.int32)[None, :]
    return {
        "x": nrm(ks[0], (BATCH, SEQ, D), 1.0),
        "c": nrm(ks[1], (BATCH, D), 1.0),
        "positions": positions,
        "ada_w": nrm(ks[3], (DEPTH, D, 6 * D), 0.5 * D ** -0.5),
        "ada_b": nrm(ks[4], (DEPTH, 6 * D), 0.02),
        "norm1_g": gain(ks[5], (DEPTH, D)),
        "norm2_g": gain(ks[6], (DEPTH, D)),
        "hyb_w_in": nrm(ks[7], (N_EVEN, D, HYB_IN), D ** -0.5),
        "mla_cq_norm_g": gain(ks[8], (N_EVEN, MLA_Q_RANK)),
        "mla_ckv_norm_g": gain(ks[9], (N_EVEN, MLA_KV_RANK)),
        "mla_w_uq": nrm(ks[10], (N_EVEN, MLA_Q_RANK, MLA_HEADS * MLA_QK_DIM), MLA_Q_RANK ** -0.5),
        "mla_w_ukv": nrm(ks[11], (N_EVEN, MLA_KV_RANK, MLA_HEADS * (MLA_NOPE_DIM + MLA_V_DIM)), MLA_KV_RANK ** -0.5),
        "mla_q_head_g": gain(ks[12], (N_EVEN, MLA_QK_DIM)),
        "mla_k_head_g": gain(ks[13], (N_EVEN, MLA_QK_DIM)),
        "ret_norm_g": gain(ks[14], (N_EVEN, RET_HEADS, RET_V_DIM)),
        "hyb_w_out": nrm(ks[15], (N_EVEN, HYB_MIX, D), HYB_MIX ** -0.5),
        "swa_w_qkv": nrm(ks[16], (N_ODD, D, SWA_QKV), D ** -0.5),
        "swa_b_qkv": nrm(ks[17], (N_ODD, SWA_QKV), 0.02),
        "swa_q_head_g": gain(ks[18], (N_ODD, SWA_HEAD_DIM)),
        "swa_k_head_g": gain(ks[19], (N_ODD, SWA_HEAD_DIM)),
        "swa_sinks": nrm(ks[20], (N_ODD, SWA_HEADS), 0.5),
        "swa_w_out": nrm(ks[21], (N_ODD, SWA_HEADS * SWA_HEAD_DIM, D), (SWA_HEADS * SWA_HEAD_DIM) ** -0.5),
        "swa_b_out": nrm(ks[22], (N_ODD, D), 0.02),
        "router_w": nrm(ks[23], (DEPTH, D, N_EXPERTS), D ** -0.5),
        "router_b": nrm(ks[24], (DEPTH, N_EXPERTS), 0.01),
        "exp_w_gu": nrm(ks[25], (DEPTH, N_EXPERTS, D, 2 * D_EXPERT), D ** -0.5),
        "exp_b_gu": nrm(ks[26], (DEPTH, N_EXPERTS, 2 * D_EXPERT), 0.02),
        "exp_w_down": nrm(ks[27], (DEPTH, N_EXPERTS, D_EXPERT, D), D_EXPERT ** -0.5),
        "exp_b_down": nrm(ks[28], (DEPTH, N_EXPERTS, D), 0.02),
    }


def reference(x, c, positions, ada_w, ada_b, norm1_g, norm2_g, hyb_w_in, mla_cq_norm_g,
              mla_ckv_norm_g, mla_w_uq, mla_w_ukv, mla_q_head_g, mla_k_head_g, ret_norm_g,
              hyb_w_out, swa_w_qkv, swa_b_qkv, swa_q_head_g, swa_k_head_g, swa_sinks,
              swa_w_out, swa_b_out, router_w, router_b, exp_w_gu, exp_b_gu, exp_w_down,
              exp_b_down):
    c_act = jax.nn.silu(c)
    for layer in range(DEPTH):
        mod = (c_act @ ada_w[layer] + ada_b[layer])[:, None, :]
        shift1, scale1, gate1, shift2, scale2, gate2 = jnp.split(mod, 6, axis=-1)
        j = layer // 2
        h = rms_norm(x, norm1_g[layer]) * (1.0 + scale1) + shift1
        if layer % 2 == 0:
            mix = mla_retention_mixer(h, positions, hyb_w_in[j], mla_cq_norm_g[j], mla_ckv_norm_g[j],
                                      mla_w_uq[j], mla_w_ukv[j], mla_q_head_g[j], mla_k_head_g[j],
                                      ret_norm_g[j], hyb_w_out[j])
        else:
            mix = swa_mixer(h, positions, swa_w_qkv[j], swa_b_qkv[j], swa_q_head_g[j],
                            swa_k_head_g[j], swa_sinks[j], swa_w_out[j], swa_b_out[j])
        x = x + gate1 * mix
        h = rms_norm(x, norm2_g[layer]) * (1.0 + scale2) + shift2
        x = x + gate2 * moe_ffn(h, router_w[layer], router_b[layer], exp_w_gu[layer], exp_b_gu[layer],
                                exp_w_down[layer], exp_b_down[layer])
    return x
```

```python
import functools

import jax
import jax.numpy as jnp
from jax import lax
from jax.experimental import pallas as pl
from jax.experimental.pallas import tpu as pltpu

F32 = jnp.float32
BF16 = jnp.bfloat16

ROPE_THETA = 10000.0
NORM_EPS = 1e-6
NEG_INF = -1e30

MLA_HEADS = 8
MLA_NOPE_DIM = 64
MLA_ROPE_DIM = 32
MLA_QK_DIM = MLA_NOPE_DIM + MLA_ROPE_DIM
MLA_V_DIM = 64
MLA_KV_RANK = 256
MLA_Q_RANK = 384

RET_HEADS = 8
RET_DIM = 64

SWA_HEADS = 16
SWA_KV_HEADS = 2
SWA_HEAD_DIM = 64
SWA_WINDOW = 128

N_EXPERTS = 32
TOP_K = 4
SWIGLU_LIMIT = 7.0
SWIGLU_ALPHA = 1.702

LANES = 128
SLAB = 8
VMEM_LIMIT = 56 * 1024 * 1024

TOKEN_TILE = 256
ATTN_TILE = 512
RET_CHUNK = 256
MOE_TILE = 512
COMBINE_TILE = 256


def _cparams(*semantics):
    return pltpu.CompilerParams(dimension_semantics=semantics, vmem_limit_bytes=VMEM_LIMIT)


def _rms(x, g):
    return x * lax.rsqrt(jnp.mean(x * x, axis=-1, keepdims=True) + NORM_EPS) * g


def _lane_iota(rows):
    return lax.broadcasted_iota(jnp.int32, (rows, LANES), 1)


def _rope_mla(x, c, s, lane):
    rolled = jnp.where((lane >> 4) == 4, pltpu.roll(x, 112, 1), pltpu.roll(x, 16, 1))
    return x * c + rolled * s


def _rope64(x, c, s, lane):
    rolled = jnp.where((lane & 32) == 0, pltpu.roll(x, 96, 1), pltpu.roll(x, 32, 1))
    return x * c + rolled * s


def _norm64(x, lane, g):
    lo = lane < 64
    sq = x * x
    s_lo = jnp.sum(jnp.where(lo, sq, 0.0), axis=-1, keepdims=True)
    s_hi = jnp.sum(jnp.where(lo, 0.0, sq), axis=-1, keepdims=True)
    inv = jnp.where(lo, lax.rsqrt(s_lo * (1.0 / 64) + NORM_EPS), lax.rsqrt(s_hi * (1.0 / 64) + NORM_EPS))
    return x * inv * g


def _ada_kernel(c_ref, w_ref, b_ref, o_ref):
    c = c_ref[...]
    a = (c * jax.nn.sigmoid(c)).astype(BF16)
    o_ref[0] = jnp.dot(a, w_ref[0].astype(BF16), preferred_element_type=F32) + b_ref[0]


def _ada_mod(c, ada_w, ada_b):
    depth, d, n6 = ada_w.shape
    b = c.shape[0]
    tn = 1536
    return pl.pallas_call(
        _ada_kernel,
        out_shape=jax.ShapeDtypeStruct((depth, b, n6), F32),
        grid=(depth, n6 // tn),
        in_specs=[pl.BlockSpec((b, d), lambda l, j: (0, 0)),
                  pl.BlockSpec((1, d, tn), lambda l, j: (l, 0, j)),
                  pl.BlockSpec((1, 1, tn), lambda l, j: (l, 0, j))],
        out_specs=pl.BlockSpec((1, b, tn), lambda l, j: (l, 0, j)),
        compiler_params=_cparams("parallel", "parallel"),
        name="ada_mod",
    )(c, ada_w, ada_b.reshape(depth, 1, n6))


def _in0_kernel(x_ref, sh_ref, sc_ref, g1_ref, win_ref, gcq_ref, gckv_ref, wuq_ref, wukv_ref,
                gq_ref, gkn_ref, gkr_ref, cm_ref, sm_ref, c64_ref, s64_ref,
                q_ref, k_ref, v_ref, rq_ref, rk_ref, rv_ref, rg_ref):
    x = x_ref[0]
    tt = x.shape[0]
    lane = _lane_iota(tt)
    h = _rms(x, g1_ref[...]) * (1.0 + sc_ref[0]) + sh_ref[0]
    proj = jnp.dot(h.astype(BF16), win_ref[...], preferred_element_type=F32)
    cm, sm = cm_ref[0], sm_ref[0]
    c64, s64 = c64_ref[0], s64_ref[0]

    cq = _rms(proj[:, 0:MLA_Q_RANK], gcq_ref[...])
    q = jnp.dot(cq.astype(BF16), wuq_ref[...], preferred_element_type=F32)
    is_nope = lane < 64
    is_rope = (lane >> 5) == 2
    for hd in range(MLA_HEADS):
        qh = q[:, hd * LANES:(hd + 1) * LANES]
        sq = qh * qh
        s_n = jnp.sum(jnp.where(is_nope, sq, 0.0), axis=-1, keepdims=True)
        s_r = jnp.sum(jnp.where(is_rope, sq, 0.0), axis=-1, keepdims=True)
        inv = jnp.where(is_nope, lax.rsqrt(s_n * (1.0 / MLA_NOPE_DIM) + NORM_EPS),
                        lax.rsqrt(s_r * (1.0 / MLA_ROPE_DIM) + NORM_EPS))
        qn = qh * inv * gq_ref[...]
        q_ref[0, :, hd * LANES:(hd + 1) * LANES] = _rope_mla(qn, cm, sm, lane).astype(BF16)

    kr = proj[:, 640:768]
    s_kr = jnp.sum(kr * kr, axis=-1, keepdims=True)
    krn = kr * lax.rsqrt(s_kr * (1.0 / MLA_ROPE_DIM) + NORM_EPS) * gkr_ref[...]
    k_rope = _rope_mla(krn, cm, sm, lane)

    ckv = _rms(proj[:, MLA_Q_RANK:640], gckv_ref[...])
    kv = jnp.dot(ckv.astype(BF16), wukv_ref[...], preferred_element_type=F32)
    for hd in range(MLA_HEADS):
        kh = kv[:, hd * LANES:(hd + 1) * LANES]
        s_k = jnp.sum(kh * kh, axis=-1, keepdims=True)
        kn = kh * lax.rsqrt(s_k * (1.0 / MLA_NOPE_DIM) + NORM_EPS) * gkn_ref[...]
        k_ref[0, :, hd * LANES:(hd + 1) * LANES] = (kn + k_rope).astype(BF16)
    v_ref[0] = kv[:, 1024:1536].astype(BF16)

    for blk in range(4):
        lo, hi = 768 + blk * LANES, 768 + (blk + 1) * LANES
        rq_ref[0, :, blk * LANES:(blk + 1) * LANES] = _rope64(proj[:, lo:hi], c64, s64, lane).astype(BF16)
        rk = _rope64(proj[:, lo + 512:hi + 512], c64, s64, lane) * (RET_DIM ** -0.5)
        rk_ref[0, :, blk * LANES:(blk + 1) * LANES] = rk.astype(BF16)
    rv_ref[0] = proj[:, 1792:2304].astype(BF16)
    rg = proj[:, 2304:2816]
    rg_ref[0] = (rg * jax.nn.sigmoid(rg)).astype(BF16)


def _in0(x, modr, layer, g1, win_p, gcq, gckv, wuq_p, wukv_p, gq, gkn, gkr, cm, sm, c64, s64):
    b, s, d = x.shape
    tt = TOKEN_TILE
    nw = win_p.shape[1]
    row = lambda bi, i: (bi, i, 0)
    const = lambda bi, i: (0, 0)
    mod_idx = lambda which: (lambda bi, i: ((layer * b + bi) * 6 + which, 0, 0))
    outs = [jax.ShapeDtypeStruct((b, s, 1024), BF16), jax.ShapeDtypeStruct((b, s, 1024), BF16)] + \
           [jax.ShapeDtypeStruct((b, s, 512), BF16)] * 5
    return pl.pallas_call(
        _in0_kernel,
        out_shape=outs,
        grid=(b, s // tt),
        in_specs=[pl.BlockSpec((1, tt, d), row),
                  pl.BlockSpec((1, 1, d), mod_idx(0)),
                  pl.BlockSpec((1, 1, d), mod_idx(1)),
                  pl.BlockSpec((1, d), const),
                  pl.BlockSpec((d, nw), const),
                  pl.BlockSpec((1, MLA_Q_RANK), const),
                  pl.BlockSpec((1, MLA_KV_RANK), const),
                  pl.BlockSpec(wuq_p.shape, const),
                  pl.BlockSpec(wukv_p.shape, const),
                  pl.BlockSpec((1, LANES), const),
                  pl.BlockSpec((1, LANES), const),
                  pl.BlockSpec((1, LANES), const),
                  pl.BlockSpec((1, tt, LANES), row),
                  pl.BlockSpec((1, tt, LANES), row),
                  pl.BlockSpec((1, tt, LANES), row),
                  pl.BlockSpec((1, tt, LANES), row)],
        out_specs=[pl.BlockSpec((1, tt, 1024), row), pl.BlockSpec((1, tt, 1024), row)] +
                  [pl.BlockSpec((1, tt, 512), row)] * 5,
        compiler_params=_cparams("parallel", "parallel"),
        name="hyb_in",
    )(x, modr, modr, g1, win_p, gcq, gckv, wuq_p, wukv_p, gq, gkn, gkr, cm, sm, c64, s64)


def _mla_kernel(q_ref, k_ref, v_ref, o_ref, m_sc, l_sc, acc_sc, *, scale):
    qi, ki = pl.program_id(1), pl.program_id(2)
    tq, tk = q_ref.shape[1], k_ref.shape[1]

    @pl.when(ki == 0)
    def _():
        m_sc[...] = jnp.full(m_sc.shape, NEG_INF, F32)
        l_sc[...] = jnp.zeros(l_sc.shape, F32)
        acc_sc[...] = jnp.zeros(acc_sc.shape, F32)

    def step(masked):
        if masked:
            row = lax.broadcasted_iota(jnp.int32, (tq, tk), 0)
            col = lax.broadcasted_iota(jnp.int32, (tq, tk), 1)
            keep = col <= row
        for hd in range(MLA_HEADS):
            qh = q_ref[0, :, hd * LANES:(hd + 1) * LANES]
            kh = k_ref[0, :, hd * LANES:(hd + 1) * LANES]
            s = lax.dot_general(qh, kh, (((1,), (1,)), ((), ())), preferred_element_type=F32) * scale
            if masked:
                s = jnp.where(keep, s, NEG_INF)
            m_old = m_sc[hd]
            m_new = jnp.maximum(m_old, jnp.max(s, axis=-1, keepdims=True))
            p = jnp.exp(s - m_new)
            alpha = jnp.exp(m_old - m_new)
            l_sc[hd] = alpha * l_sc[hd] + jnp.sum(p, axis=-1, keepdims=True)
            vh = v_ref[0, :, hd * MLA_V_DIM:(hd + 1) * MLA_V_DIM]
            acc_sc[hd] = alpha * acc_sc[hd] + jnp.dot(p.astype(BF16), vh, preferred_element_type=F32)
            m_sc[hd] = m_new

    @pl.when(ki < qi)
    def _():
        step(False)

    @pl.when(ki == qi)
    def _():
        step(True)
        o = jnp.concatenate([acc_sc[hd] / l_sc[hd] for hd in range(MLA_HEADS)], axis=-1)
        o_ref[0] = o.astype(BF16)


def _mla_attention(q, k, v):
    b, s, _ = q.shape
    t = min(ATTN_TILE, s)
    n = s // t
    kern = functools.partial(_mla_kernel, scale=MLA_QK_DIM ** -0.5)
    return pl.pallas_call(
        kern,
        out_shape=jax.ShapeDtypeStruct((b, s, MLA_HEADS * MLA_V_DIM), BF16),
        grid=(b, n, n),
        in_specs=[pl.BlockSpec((1, t, 1024), lambda bi, qi, ki: (bi, qi, 0)),
                  pl.BlockSpec((1, t, 1024), lambda bi, qi, ki: (bi, jnp.minimum(ki, qi), 0)),
                  pl.BlockSpec((1, t, 512), lambda bi, qi, ki: (bi, jnp.minimum(ki, qi), 0))],
        out_specs=pl.BlockSpec((1, t, 512), lambda bi, qi, ki: (bi, qi, 0)),
        scratch_shapes=[pltpu.VMEM((MLA_HEADS, t, 1), F32), pltpu.VMEM((MLA_HEADS, t, 1), F32),
                        pltpu.VMEM((MLA_HEADS, t, MLA_V_DIM), F32)],
        compiler_params=_cparams("parallel", "parallel", "arbitrary"),
        name="mla_attn",
    )(q, k, v)


def _ret_kernel(rq_ref, rk_ref, rv_ref, rg_ref, dm_ref, qd_ref, kd_ref, cd_ref, bd_ref, gn_ref,
                o_ref, st_sc):
    @pl.when(pl.program_id(1) == 0)
    def _():
        st_sc[...] = jnp.zeros(st_sc.shape, F32)

    c = rq_ref.shape[1]
    lane = _lane_iota(c)
    lo = lane < 64
    for pr in range(RET_HEADS // 2):
        sl = slice(pr * LANES, (pr + 1) * LANES)
        q2, k2, v2 = rq_ref[0, :, sl], rk_ref[0, :, sl], rv_ref[0, :, sl]
        inner = []
        for sub in range(2):
            qm = jnp.where(lo if sub == 0 else jnp.logical_not(lo), q2, jnp.zeros_like(q2))
            s = lax.dot_general(qm, k2, (((1,), (1,)), ((), ())), preferred_element_type=F32)
            sd = (s * dm_ref[2 * pr + sub]).astype(BF16)
            inner.append(jnp.dot(sd, v2, preferred_element_type=F32))
        state = st_sc[pr]
        qdec = (q2.astype(F32) * qd_ref[:, sl]).astype(BF16)
        cross = jnp.dot(qdec, state.astype(BF16), preferred_element_type=F32)
        y = jnp.where(lo, inner[0], inner[1]) + cross
        kdec = (k2.astype(F32) * kd_ref[:, sl]).astype(BF16)
        kv = lax.dot_general(kdec, v2, (((0,), (0,)), ((), ())), preferred_element_type=F32)
        st_sc[pr] = state * cd_ref[pr] + kv * bd_ref[...]
        mu = jnp.where(lo, jnp.sum(jnp.where(lo, y, 0.0), axis=-1, keepdims=True),
                       jnp.sum(jnp.where(lo, 0.0, y), axis=-1, keepdims=True)) * (1.0 / RET_DIM)
        yc = y - mu
        sq = yc * yc
        var = jnp.where(lo, jnp.sum(jnp.where(lo, sq, 0.0), axis=-1, keepdims=True),
                        jnp.sum(jnp.where(lo, 0.0, sq), axis=-1, keepdims=True)) * (1.0 / RET_DIM)
        yn = yc * lax.rsqrt(var + NORM_EPS) * gn_ref[:, sl]
        o_ref[0, :, sl] = (yn * rg_ref[0, :, sl].astype(F32)).astype(BF16)


def _retention(rq, rk, rv, rg, ret_norm_g):
    b, s, w = rq.shape
    c = min(RET_CHUNK, s)
    log_gamma = jnp.log1p(-jnp.exp2(-5.0 - jnp.arange(RET_HEADS, dtype=F32)))
    idx = jnp.arange(c, dtype=F32)
    diff = idx[:, None] - idx[None, :]
    dmat = jnp.where(diff >= 0, jnp.exp(log_gamma[:, None, None] * jnp.maximum(diff, 0.0)), 0.0)
    per_lane = lambda t: jnp.repeat(t, RET_DIM, axis=-1)
    qdec = per_lane(jnp.exp(log_gamma[None, :] * (idx + 1.0)[:, None]))
    kdec = per_lane(jnp.exp(log_gamma[None, :] * (c - 1.0 - idx)[:, None]))
    cdec = jnp.repeat(jnp.exp(log_gamma * c), RET_DIM).reshape(RET_HEADS // 2, LANES, 1)
    cdec = jnp.broadcast_to(cdec, (RET_HEADS // 2, LANES, LANES))
    half = jnp.arange(LANES) // RET_DIM
    bdiag = (half[:, None] == half[None, :]).astype(F32)
    gn = ret_norm_g.reshape(1, w)
    row = lambda bi, i: (bi, i, 0)
    c2 = lambda bi, i: (0, 0)
    c3 = lambda bi, i: (0, 0, 0)
    return pl.pallas_call(
        _ret_kernel,
        out_shape=jax.ShapeDtypeStruct((b, s, w), BF16),
        grid=(b, s // c),
        in_specs=[pl.BlockSpec((1, c, w), row)] * 4 +
                 [pl.BlockSpec((RET_HEADS, c, c), c3), pl.BlockSpec((c, w), c2), pl.BlockSpec((c, w), c2),
                  pl.BlockSpec((RET_HEADS // 2, LANES, LANES), c3), pl.BlockSpec((LANES, LANES), c2),
                  pl.BlockSpec((1, w), c2)],
        out_specs=pl.BlockSpec((1, c, w), row),
        scratch_shapes=[pltpu.VMEM((RET_HEADS // 2, LANES, LANES), F32)],
        compiler_params=_cparams("parallel", "arbitrary"),
        name="retention",
    )(rq, rk, rv, rg, dmat, qdec, kdec, cdec, bdiag, gn)


def _in1_kernel(x_ref, sh_ref, sc_ref, g1_ref, w_ref, b_ref, gq_ref, gk_ref, c64_ref, s64_ref,
                q_ref, k_ref, v_ref):
    x = x_ref[0]
    tt = x.shape[0]
    lane = _lane_iota(tt)
    h = _rms(x, g1_ref[...]) * (1.0 + sc_ref[0]) + sh_ref[0]
    qkv = jnp.dot(h.astype(BF16), w_ref[...], preferred_element_type=F32) + b_ref[...]
    c64, s64 = c64_ref[0], s64_ref[0]
    for blk in range(8):
        sl = slice(blk * LANES, (blk + 1) * LANES)
        q_ref[0, :, sl] = _rope64(_norm64(qkv[:, sl], lane, gq_ref[...]), c64, s64, lane).astype(BF16)
    for blk in range(2):
        sl = slice(1024 + blk * LANES, 1024 + (blk + 1) * LANES)
        kn = _rope64(_norm64(qkv[:, sl], lane, gk_ref[...]), c64, s64, lane)
        k_ref[0, :, blk * LANES:(blk + 1) * LANES] = kn.astype(BF16)
    v_ref[0] = qkv[:, 1280:1536].astype(BF16)


def _in1(x, modr, layer, g1, w_p, b_p, gq, gk, c64, s64):
    b, s, d = x.shape
    tt = TOKEN_TILE
    nw = w_p.shape[1]
    row = lambda bi, i: (bi, i, 0)
    const = lambda bi, i: (0, 0)
    mod_idx = lambda which: (lambda bi, i: ((layer * b + bi) * 6 + which, 0, 0))
    return pl.pallas_call(
        _in1_kernel,
        out_shape=[jax.ShapeDtypeStruct((b, s, 1024), BF16), jax.ShapeDtypeStruct((b, s, 256), BF16),
                   jax.ShapeDtypeStruct((b, s, 256), BF16)],
        grid=(b, s // tt),
        in_specs=[pl.BlockSpec((1, tt, d), row),
                  pl.BlockSpec((1, 1, d), mod_idx(0)),
                  pl.BlockSpec((1, 1, d), mod_idx(1)),
                  pl.BlockSpec((1, d), const),
                  pl.BlockSpec((d, nw), const),
                  pl.BlockSpec((1, nw), const),
                  pl.BlockSpec((1, LANES), const),
                  pl.BlockSpec((1, LANES), const),
                  pl.BlockSpec((1, tt, LANES), row),
                  pl.BlockSpec((1, tt, LANES), row)],
        out_specs=[pl.BlockSpec((1, tt, 1024), row), pl.BlockSpec((1, tt, 256), row),
                   pl.BlockSpec((1, tt, 256), row)],
        compiler_params=_cparams("parallel", "parallel"),
        name="swa_in",
    )(x, modr, modr, g1, w_p, b_p, gq, gk, c64, s64)


def _swa_kernel(sink_ref, q_ref, kp_ref, kc_ref, vp_ref, vc_ref, o_ref, *, scale):
    n = pl.program_id(1)
    w = q_ref.shape[1]
    lane = _lane_iota(w)
    lo = lane < 64
    i = lax.broadcasted_iota(jnp.int32, (w, 2 * w), 0)
    j = lax.broadcasted_iota(jnp.int32, (w, 2 * w), 1)
    rel = i + w - j
    first_key = jnp.where(n > 0, 0, w)
    keep = (rel >= 0) & (rel < w) & (j >= first_key)
    for g in range(SWA_KV_HEADS):
        gs = slice(g * LANES, (g + 1) * LANES)
        kd = jnp.concatenate([kp_ref[0, :, gs], kc_ref[0, :, gs]], axis=0)
        vd = jnp.concatenate([vp_ref[0, :, gs], vc_ref[0, :, gs]], axis=0)
        for pr in range(SWA_HEADS // SWA_KV_HEADS // 2):
            blk = g * 4 + pr
            q2 = q_ref[0, :, blk * LANES:(blk + 1) * LANES]
            outs = []
            for sub in range(2):
                qm = jnp.where(lo if sub == 0 else jnp.logical_not(lo), q2, jnp.zeros_like(q2))
                s = lax.dot_general(qm, kd, (((1,), (1,)), ((), ())), preferred_element_type=F32) * scale
                s = jnp.where(keep, s, NEG_INF)
                sink = sink_ref[2 * blk + sub]
                m = jnp.maximum(jnp.max(s, axis=-1, keepdims=True), sink)
                e = jnp.exp(s - m)
                denom = jnp.sum(e, axis=-1, keepdims=True) + jnp.exp(sink - m)
                p = (e / denom).astype(BF16)
                outs.append(jnp.dot(p, vd, preferred_element_type=F32))
            o_ref[0, :, blk * LANES:(blk + 1) * LANES] = jnp.where(lo, outs[0], outs[1]).astype(BF16)


def _swa_attention(q, kdup, vdup, sinks):
    b, s, _ = q.shape
    w = SWA_WINDOW
    kern = functools.partial(_swa_kernel, scale=SWA_HEAD_DIM ** -0.5)
    cur = lambda bi, n, sk: (bi, n, 0)
    prev = lambda bi, n, sk: (bi, jnp.maximum(n - 1, 0), 0)
    return pl.pallas_call(
        kern,
        out_shape=jax.ShapeDtypeStruct((b, s, 1024), BF16),
        grid_spec=pltpu.PrefetchScalarGridSpec(
            num_scalar_prefetch=1,
            grid=(b, s // w),
            in_specs=[pl.BlockSpec((1, w, 1024), cur),
                      pl.BlockSpec((1, w, 256), prev), pl.BlockSpec((1, w, 256), cur),
                      pl.BlockSpec((1, w, 256), prev), pl.BlockSpec((1, w, 256), cur)],
            out_specs=pl.BlockSpec((1, w, 1024), cur)),
        compiler_params=_cparams("parallel", "parallel"),
        name="swa_attn",
    )(sinks, q, kdup, kdup, vdup, vdup)


def _out_kernel(*refs, n_in, has_bias):
    acts = refs[:n_in]
    ws = refs[n_in:2 * n_in]
    pos = 2 * n_in
    bias_ref = refs[pos] if has_bias else None
    pos += 1 if has_bias else 0
    x_ref, g1_ref, sh_ref, sc_ref, gn_ref, rwh_ref, rwl_ref, rb_ref = refs[pos:pos + 8]
    x1_ref, h2_ref, gate_ref, idx_ref = refs[pos + 8:pos + 12]

    mix = jnp.dot(acts[0][0], ws[0][...], preferred_element_type=F32)
    for a_ref, w_ref in zip(acts[1:], ws[1:]):
        mix = mix + jnp.dot(a_ref[0], w_ref[...], preferred_element_type=F32)
    if has_bias:
        mix = mix + bias_ref[...]
    x1 = x_ref[0] + g1_ref[0] * mix
    x1_ref[0] = x1
    tt = x1.shape[0]
    h2 = _rms(x1, gn_ref[...]) * (1.0 + sc_ref[0]) + sh_ref[0]
    for s in range(SLAB):
        h2_ref[pl.ds(s, tt, stride=SLAB), :] = h2[:, s * LANES:(s + 1) * LANES]

    h_hi = h2.astype(BF16)
    h_lo = (h2 - h_hi.astype(F32)).astype(BF16)
    logits = (jnp.dot(h_hi, rwh_ref[...], preferred_element_type=F32)
              + jnp.dot(h_lo, rwh_ref[...], preferred_element_type=F32)
              + jnp.dot(h_hi, rwl_ref[...], preferred_element_type=F32)) + rb_ref[...]
    lane = _lane_iota(tt)
    lane_f = lane.astype(F32)
    cur = jnp.where(lane < N_EXPERTS, logits, -jnp.inf)
    vals, idxs = [], []
    for _ in range(TOP_K):
        m = jnp.max(cur, axis=-1, keepdims=True)
        first = jnp.min(jnp.where(cur == m, lane_f, float(LANES)), axis=-1, keepdims=True)
        vals.append(m)
        idxs.append(first)
        cur = jnp.where(lane_f == first, -jnp.inf, cur)
    es = [jnp.exp(v - vals[0]) for v in vals]
    denom = es[0] + es[1] + es[2] + es[3]
    gate = jnp.zeros((tt, LANES), F32)
    sel = jnp.zeros((tt, LANES), F32)
    for k in range(TOP_K):
        gate = jnp.where(lane == k, es[k] / denom, gate)
        sel = jnp.where(lane == k, idxs[k], sel)
    gate_ref[0] = gate
    idx_ref[0] = sel.astype(jnp.int32)


def _out_router(acts, ws, bias, x, modr, layer, gn, rw_hi, rw_lo, rb):
    b, s, d = x.shape
    tt = TOKEN_TILE
    nt = s // tt
    row = lambda bi, i: (bi, i, 0)
    const = lambda bi, i: (0, 0)
    mod_idx = lambda which: (lambda bi, i: ((layer * b + bi) * 6 + which, 0, 0))
    kern = functools.partial(_out_kernel, n_in=len(acts), has_bias=bias is not None)
    in_specs = [pl.BlockSpec((1, tt, a.shape[2]), row) for a in acts] + \
               [pl.BlockSpec(w.shape, const) for w in ws]
    args = list(acts) + list(ws)
    if bias is not None:
        in_specs.append(pl.BlockSpec((1, d), const))
        args.append(bias)
    in_specs += [pl.BlockSpec((1, tt, d), row),
                 pl.BlockSpec((1, 1, d), mod_idx(2)), pl.BlockSpec((1, 1, d), mod_idx(3)),
                 pl.BlockSpec((1, 1, d), mod_idx(4)),
                 pl.BlockSpec((1, d), const), pl.BlockSpec((d, LANES), const), pl.BlockSpec((d, LANES), const),
                 pl.BlockSpec((1, LANES), const)]
    args += [x, modr, modr, modr, gn, rw_hi, rw_lo, rb]
    return pl.pallas_call(
        kern,
        out_shape=[jax.ShapeDtypeStruct((b, s, d), F32),
                   jax.ShapeDtypeStruct((b * s * SLAB, LANES), F32),
                   jax.ShapeDtypeStruct((b, s, LANES), F32),
                   jax.ShapeDtypeStruct((b, s, LANES), jnp.int32)],
        grid=(b, nt),
        in_specs=in_specs,
        out_specs=[pl.BlockSpec((1, tt, d), row),
                   pl.BlockSpec((tt * SLAB, LANES), lambda bi, i: (bi * nt + i, 0)),
                   pl.BlockSpec((1, tt, LANES), row), pl.BlockSpec((1, tt, LANES), row)],
        compiler_params=_cparams("parallel", "parallel"),
        name="out_router",
    )(*args)


def _route(top_i, n_tok):
    tm = MOE_TILE
    m = n_tok * TOP_K
    n_tiles = m // tm + N_EXPERTS
    e_flat = top_i.reshape(m)
    order = jnp.argsort(e_flat, stable=True).astype(jnp.int32)
    counts = jnp.sum((e_flat[:, None] == jnp.arange(N_EXPERTS, dtype=jnp.int32)[None, :]).astype(jnp.int32), axis=0)
    start = jnp.cumsum(counts) - counts
    ntile_e = (counts + tm - 1) // tm
    tile_end = jnp.cumsum(ntile_e)
    tile_start = tile_end - ntile_e
    n_used = tile_end[-1]
    t = jnp.arange(n_tiles, dtype=jnp.int32)
    tile_e = jnp.minimum(jnp.searchsorted(tile_end, t, side="right"), N_EXPERTS - 1).astype(jnp.int32)
    r = jnp.arange(tm, dtype=jnp.int32)
    rank = (t - tile_start[tile_e])[:, None] * tm + r[None, :]
    cnt = counts[tile_e][:, None]
    valid = (t[:, None] < n_used) & (rank < cnt)
    src = jnp.clip(start[tile_e][:, None] + rank, 0, m - 1)
    flat = order[src]
    slot_tok = jnp.where(valid, flat // TOP_K, 0).astype(jnp.int32)
    slot_dst = jnp.where(valid, flat, 0).astype(jnp.int32)
    tile_nv = jnp.sum(valid.astype(jnp.int32), axis=1)
    return (tile_e, n_used.reshape(1).astype(jnp.int32), tile_nv,
            slot_tok.reshape(n_tiles, 1, tm), slot_dst.reshape(n_tiles, 1, tm))


def _ffn_kernel(te_ref, nu_ref, nv_ref, tok_ref, tokn_ref, dst_ref, h_hbm, wgu_ref, bgu_ref, wd_ref, bd_ref,
                y_hbm, xbuf, obuf, gsem, ssem):
    t = pl.program_id(0)
    nu = nu_ref[0]
    tm = tok_ref.shape[2]
    slot = lax.rem(t, 2)

    def gather_copy(idx_ref, r, sl):
        tok = idx_ref[0, 0, r]
        return pltpu.make_async_copy(h_hbm.at[pl.ds(pl.multiple_of(tok * SLAB, SLAB), SLAB), :],
                                     xbuf.at[sl, pl.ds(pl.multiple_of(r * SLAB, SLAB), SLAB), :],
                                     gsem.at[sl])

    def scatter_copy(r, sl):
        dst = dst_ref[0, 0, r]
        return pltpu.make_async_copy(obuf.at[sl, pl.ds(pl.multiple_of(r * SLAB, SLAB), SLAB), :],
                                     y_hbm.at[pl.ds(pl.multiple_of(dst * SLAB, SLAB), SLAB), :],
                                     ssem.at[sl])

    def issue_gather(idx_ref, sl):
        def body(r, carry):
            gather_copy(idx_ref, r, sl).start()
            return carry
        lax.fori_loop(0, tm, body, 0, unroll=8)

    def wait_rows(buf, sem, sl, rows):
        view = buf.at[sl, pl.ds(0, rows * SLAB), :]
        pltpu.make_async_copy(view, view, sem.at[sl]).wait()

    @pl.when(t < nu)
    def _():
        @pl.when(t == 0)
        def _():
            issue_gather(tok_ref, 0)

        @pl.when(t + 1 < nu)
        def _():
            issue_gather(tokn_ref, 1 - slot)

        wait_rows(xbuf, gsem, slot, tm)
        x = jnp.concatenate([xbuf[slot, pl.ds(s, tm, stride=SLAB), :].astype(BF16) for s in range(SLAB)],
                            axis=-1)
        gu = jnp.dot(x, wgu_ref[0], preferred_element_type=F32) + bgu_ref[0]
        de = gu.shape[1] // 2
        glu = jnp.minimum(gu[:, :de], SWIGLU_LIMIT)
        lin = jnp.clip(gu[:, de:], -SWIGLU_LIMIT, SWIGLU_LIMIT)
        act = glu * jax.nn.sigmoid(SWIGLU_ALPHA * glu) * (lin + 1.0)
        out = jnp.dot(act.astype(BF16), wd_ref[0], preferred_element_type=F32) + bd_ref[0]

        @pl.when(t >= 2)
        def _():
            wait_rows(obuf, ssem, slot, nv_ref[t - 2])

        for s in range(SLAB):
            obuf[slot, pl.ds(s, tm, stride=SLAB), :] = out[:, s * LANES:(s + 1) * LANES]

        nv = nv_ref[t]
        groups = lax.shift_right_logical(nv, 3)

        def sbody8(g, carry):
            for u in range(8):
                scatter_copy(g * 8 + u, slot).start()
            return carry
        lax.fori_loop(0, groups, sbody8, 0)

        def sbody(r, carry):
            scatter_copy(r, slot).start()
            return carry
        lax.fori_loop(groups * 8, nv, sbody, 0)

        @pl.when(t == nu - 1)
        def _():
            wait_rows(obuf, ssem, slot, nv)

            @pl.when(t >= 1)
            def _():
                wait_rows(obuf, ssem, 1 - slot, nv_ref[jnp.maximum(t - 1, 0)])


def _moe_ffn(h2_slab, tile_e, n_used, tile_nv, slot_tok, slot_dst, wgu, bgu, wd, bd):
    n_tiles, _, tm = slot_tok.shape
    n_exp, d, de2 = wgu.shape
    rows_out = h2_slab.shape[0] * TOP_K
    cur = lambda t, te, nu, nv: (t, 0, 0)
    nxt = lambda t, te, nu, nv: (jnp.minimum(t + 1, n_tiles - 1), 0, 0)
    w_idx = lambda t, te, nu, nv: (te[jnp.minimum(t, nu[0] - 1)], 0, 0)
    smem = functools.partial(pl.BlockSpec, memory_space=pltpu.SMEM)
    return pl.pallas_call(
        _ffn_kernel,
        out_shape=jax.ShapeDtypeStruct((rows_out, LANES), F32),
        grid_spec=pltpu.PrefetchScalarGridSpec(
            num_scalar_prefetch=3,
            grid=(n_tiles,),
            in_specs=[smem((1, 1, tm), cur), smem((1, 1, tm), nxt), smem((1, 1, tm), cur),
                      pl.BlockSpec(memory_space=pl.ANY),
                      pl.BlockSpec((1, d, de2), w_idx), pl.BlockSpec((1, 1, de2), w_idx),
                      pl.BlockSpec((1, de2 // 2, d), w_idx), pl.BlockSpec((1, 1, d), w_idx)],
            out_specs=pl.BlockSpec(memory_space=pl.ANY),
            scratch_shapes=[pltpu.VMEM((2, tm * SLAB, LANES), F32), pltpu.VMEM((2, tm * SLAB, LANES), F32),
                            pltpu.SemaphoreType.DMA((2,)), pltpu.SemaphoreType.DMA((2,))]),
        compiler_params=_cparams("arbitrary"),
        name="moe_ffn",
    )(tile_e, n_used, tile_nv, slot_tok, slot_tok, slot_dst, h2_slab, wgu, bgu.reshape(n_exp, 1, de2), wd,
      bd.reshape(n_exp, 1, d))


def _combine_kernel(y_ref, x1_ref, g2_ref, gate_ref, o_ref):
    tc = x1_ref.shape[1]
    gate = gate_ref[0]
    gk = [gate[:, k:k + 1] for k in range(TOP_K)]
    x1 = x1_ref[0]
    g2 = g2_ref[0]
    for s in range(SLAB):
        acc = gk[0] * y_ref[pl.ds(s, tc, stride=TOP_K * SLAB), :]
        for k in range(1, TOP_K):
            acc = acc + gk[k] * y_ref[pl.ds(k * SLAB + s, tc, stride=TOP_K * SLAB), :]
        sl = slice(s * LANES, (s + 1) * LANES)
        o_ref[0, :, sl] = x1[:, sl] + g2[:, sl] * acc


def _combine(y_slab, x1, modr, layer, gates):
    b, s, d = x1.shape
    tc = COMBINE_TILE
    nt = s // tc
    row = lambda bi, i: (bi, i, 0)
    return pl.pallas_call(
        _combine_kernel,
        out_shape=jax.ShapeDtypeStruct((b, s, d), F32),
        grid=(b, nt),
        in_specs=[pl.BlockSpec((tc * TOP_K * SLAB, LANES), lambda bi, i: (bi * nt + i, 0)),
                  pl.BlockSpec((1, tc, d), row),
                  pl.BlockSpec((1, 1, d), lambda bi, i: ((layer * b + bi) * 6 + 5, 0, 0)),
                  pl.BlockSpec((1, tc, LANES), row)],
        out_specs=pl.BlockSpec((1, tc, d), row),
        compiler_params=_cparams("parallel", "parallel"),
        name="moe_combine",
    )(y_slab, x1, modr, gates)


def _pad_cols(w, groups, width, to):
    k = w.shape[0]
    w3 = w.reshape(k, groups, width)
    return jnp.pad(w3, ((0, 0), (0, 0), (0, to - width))).reshape(k, groups * to)


def _rope_tables(positions):
    pos = positions.astype(F32)[:, :, None]
    def cs(half):
        inv = ROPE_THETA ** (-jnp.arange(half, dtype=F32) / half)
        ang = pos * inv
        return jnp.cos(ang), jnp.sin(ang)
    c16, s16 = cs(MLA_ROPE_DIM // 2)
    b, s = positions.shape
    ones, zeros = jnp.ones((b, s, 64), F32), jnp.zeros((b, s, 32), F32)
    cm = jnp.concatenate([ones, c16, c16, zeros], axis=-1)
    sm = jnp.concatenate([jnp.zeros((b, s, 64), F32), -s16, s16, zeros], axis=-1)
    c32, s32 = cs(RET_DIM // 2)
    c64 = jnp.concatenate([c32, c32, c32, c32], axis=-1)
    s64 = jnp.concatenate([-s32, s32, -s32, s32], axis=-1)
    return cm, sm, c64, s64


def _router_split(router_w, router_b):
    d = router_w.shape[0]
    w = jnp.pad(router_w, ((0, 0), (0, LANES - N_EXPERTS)))
    hi = w.astype(BF16)
    lo = (w - hi.astype(F32)).astype(BF16)
    rb = jnp.pad(router_b, (0, LANES - N_EXPERTS)).reshape(1, LANES)
    return hi, lo, rb


def _moe_layer(x1, h2_slab, gates, top_i, modr, layer, w_gu, b_gu, w_down, b_down):
    b, s, d = x1.shape
    tile_e, n_used, tile_nv, slot_tok, slot_dst = _route(top_i[..., :TOP_K], b * s)
    y_slab = _moe_ffn(h2_slab, tile_e, n_used, tile_nv, slot_tok, slot_dst,
                      w_gu.astype(BF16), b_gu, w_down.astype(BF16), b_down)
    return _combine(y_slab, x1, modr, layer, gates)


def kernel(x, c, positions, ada_w, ada_b, norm1_g, norm2_g, hyb_w_in, mla_cq_norm_g, mla_ckv_norm_g, mla_w_uq, mla_w_ukv, mla_q_head_g, mla_k_head_g, ret_norm_g, hyb_w_out, swa_w_qkv, swa_b_qkv, swa_q_head_g, swa_k_head_g, swa_sinks, swa_w_out, swa_b_out, router_w, router_b, exp_w_gu, exp_b_gu, exp_w_down, exp_b_down):
    b, s, d = x.shape
    depth = ada_w.shape[0]
    mod = _ada_mod(c, ada_w, ada_b)
    modr = mod.reshape(depth * b * 6, 1, d)
    cm, sm, c64, s64 = _rope_tables(positions)

    for layer in range(depth):
        j = layer // 2
        g1 = norm1_g[layer].reshape(1, d)
        g2 = norm2_g[layer].reshape(1, d)
        if layer % 2 == 0:
            w_in = hyb_w_in[j]
            zc = lambda n: jnp.zeros((d, n), F32)
            win_p = jnp.concatenate([w_in[:, :640], zc(64), w_in[:, 640:672], zc(32), w_in[:, 672:]],
                                    axis=1).astype(BF16)
            wuq_p = _pad_cols(mla_w_uq[j], MLA_HEADS, MLA_QK_DIM, LANES).astype(BF16)
            wukv3 = mla_w_ukv[j].reshape(MLA_KV_RANK, MLA_HEADS, MLA_NOPE_DIM + MLA_V_DIM)
            wk = jnp.pad(wukv3[:, :, :MLA_NOPE_DIM], ((0, 0), (0, 0), (0, LANES - MLA_NOPE_DIM)))
            wukv_p = jnp.concatenate([wk.reshape(MLA_KV_RANK, MLA_HEADS * LANES),
                                      wukv3[:, :, MLA_NOPE_DIM:].reshape(MLA_KV_RANK, MLA_HEADS * MLA_V_DIM)],
                                     axis=1).astype(BF16)
            z32, z64 = jnp.zeros((32,), F32), jnp.zeros((64,), F32)
            gq = jnp.concatenate([mla_q_head_g[j], z32]).reshape(1, LANES)
            gkn = jnp.concatenate([mla_k_head_g[j][:MLA_NOPE_DIM], z64]).reshape(1, LANES)
            gkr = jnp.concatenate([z64, mla_k_head_g[j][MLA_NOPE_DIM:], z32]).reshape(1, LANES)
            q, k, v, rq, rk, rv, rg = _in0(x, modr, layer, g1, win_p, mla_cq_norm_g[j].reshape(1, -1),
                                           mla_ckv_norm_g[j].reshape(1, -1), wuq_p, wukv_p, gq, gkn, gkr,
                                           cm, sm, c64, s64)
            attn = _mla_attention(q, k, v)
            ret = _retention(rq, rk, rv, rg, ret_norm_g[j])
            w_out = hyb_w_out[j].astype(BF16)
            acts, ws, bias = [attn, ret], [w_out[:512], w_out[512:]], None
        else:
            wq = swa_w_qkv[j]
            bq = swa_b_qkv[j]
            nq = SWA_HEADS * SWA_HEAD_DIM
            hd = SWA_HEAD_DIM
            dup = lambda t: jnp.concatenate([t[..., 0:hd], t[..., 0:hd], t[..., hd:2 * hd], t[..., hd:2 * hd]], axis=-1)
            w_p = jnp.concatenate([wq[:, :nq], dup(wq[:, nq:nq + 2 * hd]), dup(wq[:, nq + 2 * hd:])], axis=1).astype(BF16)
            b_p = jnp.concatenate([bq[:nq], dup(bq[nq:nq + 2 * hd]), dup(bq[nq + 2 * hd:])]).reshape(1, -1)
            gq = jnp.concatenate([swa_q_head_g[j], swa_q_head_g[j]]).reshape(1, LANES)
            gk = jnp.concatenate([swa_k_head_g[j], swa_k_head_g[j]]).reshape(1, LANES)
            q, kdup, vdup = _in1(x, modr, layer, g1, w_p, b_p, gq, gk, c64, s64)
            o = _swa_attention(q, kdup, vdup, swa_sinks[j])
            acts, ws, bias = [o], [swa_w_out[j].astype(BF16)], swa_b_out[j].reshape(1, d)
        rw_hi, rw_lo, rb = _router_split(router_w[layer], router_b[layer])
        x1, h2_slab, gates, top_i = _out_router(acts, ws, bias, x, modr, layer, g2, rw_hi, rw_lo, rb)
        x = _moe_layer(x1, h2_slab, gates, top_i, modr, layer, exp_w_gu[layer], exp_b_gu[layer],
                       exp_w_down[layer], exp_b_down[layer])
    return x
```

```python
import functools

import jax
import jax.numpy as jnp
from jax import lax
from jax.experimental import pallas as pl
from jax.experimental.pallas import tpu as pltpu

F32 = jnp.float32
BF16 = jnp.bfloat16

ROPE_THETA = 10000.0
NORM_EPS = 1e-6
NEG_INF = -1e30

MLA_HEADS = 8
MLA_NOPE_DIM = 64
MLA_ROPE_DIM = 32
MLA_QK_DIM = MLA_NOPE_DIM + MLA_ROPE_DIM
MLA_V_DIM = 64
MLA_KV_RANK = 256
MLA_Q_RANK = 384

RET_HEADS = 8
RET_DIM = 64

SWA_HEADS = 16
SWA_KV_HEADS = 2
SWA_HEAD_DIM = 64
SWA_WINDOW = 128

N_EXPERTS = 32
TOP_K = 4
SWIGLU_LIMIT = 7.0
SWIGLU_ALPHA = 1.702

LANES = 128
SLAB = 8
VMEM_LIMIT = 56 * 1024 * 1024

TOKEN_TILE = 256
ATTN_TILE = 256
RET_CHUNK = 256
MOE_TILE = 512
COMBINE_TILE = 256


def _cparams(*semantics):
    return pltpu.CompilerParams(dimension_semantics=semantics, vmem_limit_bytes=VMEM_LIMIT)


def _rms(x, g):
    return x * lax.rsqrt(jnp.mean(x * x, axis=-1, keepdims=True) + NORM_EPS) * g


def _lane_iota(rows):
    return lax.broadcasted_iota(jnp.int32, (rows, LANES), 1)


def _rope_mla(x, c, s, lane):
    rolled = jnp.where((lane >> 4) == 4, pltpu.roll(x, 112, 1), pltpu.roll(x, 16, 1))
    return x * c + rolled * s


def _rope64(x, c, s, lane):
    rolled = jnp.where((lane & 32) == 0, pltpu.roll(x, 96, 1), pltpu.roll(x, 32, 1))
    return x * c + rolled * s


def _norm64(x, lane, g):
    lo = lane < 64
    sq = x * x
    s_lo = jnp.sum(jnp.where(lo, sq, 0.0), axis=-1, keepdims=True)
    s_hi = jnp.sum(jnp.where(lo, 0.0, sq), axis=-1, keepdims=True)
    inv = jnp.where(lo, lax.rsqrt(s_lo * (1.0 / 64) + NORM_EPS), lax.rsqrt(s_hi * (1.0 / 64) + NORM_EPS))
    return x * inv * g


def _ada_kernel(c_ref, w_ref, b_ref, o_ref):
    c = c_ref[...]
    a = (c * jax.nn.sigmoid(c)).astype(BF16)
    o_ref[0] = jnp.dot(a, w_ref[0].astype(BF16), preferred_element_type=F32) + b_ref[0]


def _ada_mod(c, ada_w, ada_b):
    depth, d, n6 = ada_w.shape
    b = c.shape[0]
    tn = 1536
    return pl.pallas_call(
        _ada_kernel,
        out_shape=jax.ShapeDtypeStruct((depth, b, n6), F32),
        grid=(depth, n6 // tn),
        in_specs=[pl.BlockSpec((b, d), lambda l, j: (0, 0)),
                  pl.BlockSpec((1, d, tn), lambda l, j: (l, 0, j)),
                  pl.BlockSpec((1, 1, tn), lambda l, j: (l, 0, j))],
        out_specs=pl.BlockSpec((1, b, tn), lambda l, j: (l, 0, j)),
        compiler_params=_cparams("parallel", "parallel"),
        name="ada_mod",
    )(c, ada_w, ada_b.reshape(depth, 1, n6))


def _in0_kernel(x_ref, sh_ref, sc_ref, g1_ref, win_ref, gcq_ref, gckv_ref, wuq_ref, wukv_ref,
                gq_ref, gkn_ref, gkr_ref, cm_ref, sm_ref, c64_ref, s64_ref,
                q_ref, k_ref, v_ref, rq_ref, rk_ref, rv_ref, rg_ref):
    x = x_ref[0]
    tt = x.shape[0]
    lane = _lane_iota(tt)
    h = _rms(x, g1_ref[...]) * (1.0 + sc_ref[0]) + sh_ref[0]
    proj = jnp.dot(h.astype(BF16), win_ref[...], preferred_element_type=F32)
    cm, sm = cm_ref[0], sm_ref[0]
    c64, s64 = c64_ref[0], s64_ref[0]

    cq = _rms(proj[:, 0:MLA_Q_RANK], gcq_ref[...])
    q = jnp.dot(cq.astype(BF16), wuq_ref[...], preferred_element_type=F32)
    is_nope = lane < 64
    is_rope = (lane >> 5) == 2
    for hd in range(MLA_HEADS):
        qh = q[:, hd * LANES:(hd + 1) * LANES]
        sq = qh * qh
        s_n = jnp.sum(jnp.where(is_nope, sq, 0.0), axis=-1, keepdims=True)
        s_r = jnp.sum(jnp.where(is_rope, sq, 0.0), axis=-1, keepdims=True)
        inv = jnp.where(is_nope, lax.rsqrt(s_n * (1.0 / MLA_NOPE_DIM) + NORM_EPS),
                        lax.rsqrt(s_r * (1.0 / MLA_ROPE_DIM) + NORM_EPS))
        qn = qh * inv * gq_ref[...]
        q_ref[0, :, hd * LANES:(hd + 1) * LANES] = _rope_mla(qn, cm, sm, lane).astype(BF16)

    kr = proj[:, 640:768]
    s_kr = jnp.sum(kr * kr, axis=-1, keepdims=True)
    krn = kr * lax.rsqrt(s_kr * (1.0 / MLA_ROPE_DIM) + NORM_EPS) * gkr_ref[...]
    k_rope = _rope_mla(krn, cm, sm, lane)

    ckv = _rms(proj[:, MLA_Q_RANK:640], gckv_ref[...])
    kv = jnp.dot(ckv.astype(BF16), wukv_ref[...], preferred_element_type=F32)
    for hd in range(MLA_HEADS):
        kh = kv[:, hd * LANES:(hd + 1) * LANES]
        s_k = jnp.sum(kh * kh, axis=-1, keepdims=True)
        kn = kh * lax.rsqrt(s_k * (1.0 / MLA_NOPE_DIM) + NORM_EPS) * gkn_ref[...]
        k_ref[0, :, hd * LANES:(hd + 1) * LANES] = (kn + k_rope).astype(BF16)
    v_ref[0] = kv[:, 1024:1536].astype(BF16)

    for blk in range(4):
        lo, hi = 768 + blk * LANES, 768 + (blk + 1) * LANES
        rq_ref[0, :, blk * LANES:(blk + 1) * LANES] = _rope64(proj[:, lo:hi], c64, s64, lane).astype(BF16)
        rk = _rope64(proj[:, lo + 512:hi + 512], c64, s64, lane) * (RET_DIM ** -0.5)
        rk_ref[0, :, blk * LANES:(blk + 1) * LANES] = rk.astype(BF16)
    rv_ref[0] = proj[:, 1792:2304].astype(BF16)
    rg = proj[:, 2304:2816]
    rg_ref[0] = (rg * jax.nn.sigmoid(rg)).astype(BF16)


def _in0(x, modr, layer, g1, win_p, gcq, gckv, wuq_p, wukv_p, gq, gkn, gkr, cm, sm, c64, s64):
    b, s, d = x.shape
    tt = TOKEN_TILE
    nw = win_p.shape[1]
    row = lambda bi, i: (bi, i, 0)
    const = lambda bi, i: (0, 0)
    mod_idx = lambda which: (lambda bi, i: ((layer * b + bi) * 6 + which, 0, 0))
    outs = [jax.ShapeDtypeStruct((b, s, 1024), BF16), jax.ShapeDtypeStruct((b, s, 1024), BF16)] + \
           [jax.ShapeDtypeStruct((b, s, 512), BF16)] * 5
    return pl.pallas_call(
        _in0_kernel,
        out_shape=outs,
        grid=(b, s // tt),
        in_specs=[pl.BlockSpec((1, tt, d), row),
                  pl.BlockSpec((1, 1, d), mod_idx(0)),
                  pl.BlockSpec((1, 1, d), mod_idx(1)),
                  pl.BlockSpec((1, d), const),
                  pl.BlockSpec((d, nw), const),
                  pl.BlockSpec((1, MLA_Q_RANK), const),
                  pl.BlockSpec((1, MLA_KV_RANK), const),
                  pl.BlockSpec(wuq_p.shape, const),
                  pl.BlockSpec(wukv_p.shape, const),
                  pl.BlockSpec((1, LANES), const),
                  pl.BlockSpec((1, LANES), const),
                  pl.BlockSpec((1, LANES), const),
                  pl.BlockSpec((1, tt, LANES), row),
                  pl.BlockSpec((1, tt, LANES), row),
                  pl.BlockSpec((1, tt, LANES), row),
                  pl.BlockSpec((1, tt, LANES), row)],
        out_specs=[pl.BlockSpec((1, tt, 1024), row), pl.BlockSpec((1, tt, 1024), row)] +
                  [pl.BlockSpec((1, tt, 512), row)] * 5,
        compiler_params=_cparams("parallel", "parallel"),
        name="hyb_in",
    )(x, modr, modr, g1, win_p, gcq, gckv, wuq_p, wukv_p, gq, gkn, gkr, cm, sm, c64, s64)


def _mla_kernel(q_ref, k_ref, v_ref, o_ref, *, scale_log2e):
    qi = pl.program_id(1)
    tq = q_ref.shape[1]
    nq = k_ref.shape[1] // tq
    keep = (lax.broadcasted_iota(jnp.int32, (tq, tq), 1) <= lax.broadcasted_iota(jnp.int32, (tq, tq), 0))
    nt = (((1,), (1,)), ((), ()))

    def body(i):
        off = i * tq
        outs = []
        for hd in range(MLA_HEADS):
            hs = slice(hd * LANES, (hd + 1) * LANES)
            vs = slice(hd * MLA_V_DIM, (hd + 1) * MLA_V_DIM)
            qh = q_ref[0, :, hs]
            s_d = lax.dot_general(qh, k_ref[0, off:off + tq, hs], nt, preferred_element_type=F32)
            s_d = jnp.where(keep, s_d, NEG_INF)
            m = jnp.max(s_d, axis=-1, keepdims=True)
            if i > 0:
                s_o = lax.dot_general(qh, k_ref[0, 0:off, hs], nt, preferred_element_type=F32)
                m = jnp.maximum(m, jnp.max(s_o, axis=-1, keepdims=True))
            p_d = jnp.exp2((s_d - m) * scale_log2e)
            den = jnp.sum(p_d, axis=-1, keepdims=True)
            acc = jnp.dot(p_d.astype(BF16), v_ref[0, off:off + tq, vs], preferred_element_type=F32)
            if i > 0:
                p_o = jnp.exp2((s_o - m) * scale_log2e)
                den = den + jnp.sum(p_o, axis=-1, keepdims=True)
                acc = acc + jnp.dot(p_o.astype(BF16), v_ref[0, 0:off, vs], preferred_element_type=F32)
            outs.append(acc / den)
        o_ref[0] = jnp.concatenate(outs, axis=-1).astype(BF16)

    for i in range(nq):
        pl.when(qi == i)(functools.partial(body, i))


def _mla_attention(q, k, v):
    b, s, _ = q.shape
    t = min(ATTN_TILE, s)
    kern = functools.partial(_mla_kernel, scale_log2e=MLA_QK_DIM ** -0.5 * 1.4426950408889634)
    return pl.pallas_call(
        kern,
        out_shape=jax.ShapeDtypeStruct((b, s, MLA_HEADS * MLA_V_DIM), BF16),
        grid=(b, s // t),
        in_specs=[pl.BlockSpec((1, t, 1024), lambda bi, qi: (bi, qi, 0)),
                  pl.BlockSpec((1, s, 1024), lambda bi, qi: (bi, 0, 0)),
                  pl.BlockSpec((1, s, 512), lambda bi, qi: (bi, 0, 0))],
        out_specs=pl.BlockSpec((1, t, 512), lambda bi, qi: (bi, qi, 0)),
        compiler_params=_cparams("parallel", "arbitrary"),
        name="mla_attn",
    )(q, k, v)


def _ret_kernel(rq_ref, rk_ref, rv_ref, rg_ref, dm_ref, qd_ref, kd_ref, cd_ref, bd_ref, gn_ref,
                o_ref, st_sc):
    @pl.when(pl.program_id(1) == 0)
    def _():
        st_sc[...] = jnp.zeros(st_sc.shape, F32)

    c = rq_ref.shape[1]
    lane = _lane_iota(c)
    lo = lane < 64
    for pr in range(RET_HEADS // 2):
        sl = slice(pr * LANES, (pr + 1) * LANES)
        q2, k2, v2 = rq_ref[0, :, sl], rk_ref[0, :, sl], rv_ref[0, :, sl]
        inner = []
        for sub in range(2):
            qm = jnp.where(lo if sub == 0 else jnp.logical_not(lo), q2, jnp.zeros_like(q2))
            s = lax.dot_general(qm, k2, (((1,), (1,)), ((), ())), preferred_element_type=F32)
            sd = (s * dm_ref[2 * pr + sub]).astype(BF16)
            inner.append(jnp.dot(sd, v2, preferred_element_type=F32))
        state = st_sc[pr]
        qdec = (q2.astype(F32) * qd_ref[:, sl]).astype(BF16)
        cross = jnp.dot(qdec, state.astype(BF16), preferred_element_type=F32)
        y = jnp.where(lo, inner[0], inner[1]) + cross
        kdec = (k2.astype(F32) * kd_ref[:, sl]).astype(BF16)
        kv = lax.dot_general(kdec, v2, (((0,), (0,)), ((), ())), preferred_element_type=F32)
        st_sc[pr] = state * cd_ref[pr] + kv * bd_ref[...]
        mu = jnp.where(lo, jnp.sum(jnp.where(lo, y, 0.0), axis=-1, keepdims=True),
                       jnp.sum(jnp.where(lo, 0.0, y), axis=-1, keepdims=True)) * (1.0 / RET_DIM)
        yc = y - mu
        sq = yc * yc
        var = jnp.where(lo, jnp.sum(jnp.where(lo, sq, 0.0), axis=-1, keepdims=True),
                        jnp.sum(jnp.where(lo, 0.0, sq), axis=-1, keepdims=True)) * (1.0 / RET_DIM)
        yn = yc * lax.rsqrt(var + NORM_EPS) * gn_ref[:, sl]
        o_ref[0, :, sl] = (yn * rg_ref[0, :, sl].astype(F32)).astype(BF16)


def _retention(rq, rk, rv, rg, ret_norm_g):
    b, s, w = rq.shape
    c = min(RET_CHUNK, s)
    log_gamma = jnp.log1p(-jnp.exp2(-5.0 - jnp.arange(RET_HEADS, dtype=F32)))
    idx = jnp.arange(c, dtype=F32)
    diff = idx[:, None] - idx[None, :]
    dmat = jnp.where(diff >= 0, jnp.exp(log_gamma[:, None, None] * jnp.maximum(diff, 0.0)), 0.0)
    per_lane = lambda t: jnp.repeat(t, RET_DIM, axis=-1)
    qdec = per_lane(jnp.exp(log_gamma[None, :] * (idx + 1.0)[:, None]))
    kdec = per_lane(jnp.exp(log_gamma[None, :] * (c - 1.0 - idx)[:, None]))
    cdec = jnp.repeat(jnp.exp(log_gamma * c), RET_DIM).reshape(RET_HEADS // 2, LANES, 1)
    cdec = jnp.broadcast_to(cdec, (RET_HEADS // 2, LANES, LANES))
    half = jnp.arange(LANES) // RET_DIM
    bdiag = (half[:, None] == half[None, :]).astype(F32)
    gn = ret_norm_g.reshape(1, w)
    row = lambda bi, i: (bi, i, 0)
    c2 = lambda bi, i: (0, 0)
    c3 = lambda bi, i: (0, 0, 0)
    return pl.pallas_call(
        _ret_kernel,
        out_shape=jax.ShapeDtypeStruct((b, s, w), BF16),
        grid=(b, s // c),
        in_specs=[pl.BlockSpec((1, c, w), row)] * 4 +
                 [pl.BlockSpec((RET_HEADS, c, c), c3), pl.BlockSpec((c, w), c2), pl.BlockSpec((c, w), c2),
                  pl.BlockSpec((RET_HEADS // 2, LANES, LANES), c3), pl.BlockSpec((LANES, LANES), c2),
                  pl.BlockSpec((1, w), c2)],
        out_specs=pl.BlockSpec((1, c, w), row),
        scratch_shapes=[pltpu.VMEM((RET_HEADS // 2, LANES, LANES), F32)],
        compiler_params=_cparams("parallel", "arbitrary"),
        name="retention",
    )(rq, rk, rv, rg, dmat, qdec, kdec, cdec, bdiag, gn)


def _in1_kernel(x_ref, sh_ref, sc_ref, g1_ref, w_ref, b_ref, gq_ref, gk_ref, c64_ref, s64_ref,
                q_ref, k_ref, v_ref):
    x = x_ref[0]
    tt = x.shape[0]
    lane = _lane_iota(tt)
    h = _rms(x, g1_ref[...]) * (1.0 + sc_ref[0]) + sh_ref[0]
    qkv = jnp.dot(h.astype(BF16), w_ref[...], preferred_element_type=F32) + b_ref[...]
    c64, s64 = c64_ref[0], s64_ref[0]
    for blk in range(8):
        sl = slice(blk * LANES, (blk + 1) * LANES)
        q_ref[0, :, sl] = _rope64(_norm64(qkv[:, sl], lane, gq_ref[...]), c64, s64, lane).astype(BF16)
    for blk in range(2):
        sl = slice(1024 + blk * LANES, 1024 + (blk + 1) * LANES)
        kn = _rope64(_norm64(qkv[:, sl], lane, gk_ref[...]), c64, s64, lane)
        k_ref[0, :, blk * LANES:(blk + 1) * LANES] = kn.astype(BF16)
    v_ref[0] = qkv[:, 1280:1536].astype(BF16)


def _in1(x, modr, layer, g1, w_p, b_p, gq, gk, c64, s64):
    b, s, d = x.shape
    tt = TOKEN_TILE
    nw = w_p.shape[1]
    row = lambda bi, i: (bi, i, 0)
    const = lambda bi, i: (0, 0)
    mod_idx = lambda which: (lambda bi, i: ((layer * b + bi) * 6 + which, 0, 0))
    return pl.pallas_call(
        _in1_kernel,
        out_shape=[jax.ShapeDtypeStruct((b, s, 1024), BF16), jax.ShapeDtypeStruct((b, s, 256), BF16),
                   jax.ShapeDtypeStruct((b, s, 256), BF16)],
        grid=(b, s // tt),
        in_specs=[pl.BlockSpec((1, tt, d), row),
                  pl.BlockSpec((1, 1, d), mod_idx(0)),
                  pl.BlockSpec((1, 1, d), mod_idx(1)),
                  pl.BlockSpec((1, d), const),
                  pl.BlockSpec((d, nw), const),
                  pl.BlockSpec((1, nw), const),
                  pl.BlockSpec((1, LANES), const),
                  pl.BlockSpec((1, LANES), const),
                  pl.BlockSpec((1, tt, LANES), row),
                  pl.BlockSpec((1, tt, LANES), row)],
        out_specs=[pl.BlockSpec((1, tt, 1024), row), pl.BlockSpec((1, tt, 256), row),
                   pl.BlockSpec((1, tt, 256), row)],
        compiler_params=_cparams("parallel", "parallel"),
        name="swa_in",
    )(x, modr, modr, g1, w_p, b_p, gq, gk, c64, s64)


def _swa_kernel(sink_ref, q_ref, kp_ref, kc_ref, vp_ref, vc_ref, o_ref, *, scale):
    n = pl.program_id(1)
    w = q_ref.shape[1]
    lane = _lane_iota(w)
    lo = lane < 64
    i = lax.broadcasted_iota(jnp.int32, (w, 2 * w), 0)
    j = lax.broadcasted_iota(jnp.int32, (w, 2 * w), 1)
    rel = i + w - j
    first_key = jnp.where(n > 0, 0, w)
    keep = (rel >= 0) & (rel < w) & (j >= first_key)
    for g in range(SWA_KV_HEADS):
        gs = slice(g * LANES, (g + 1) * LANES)
        kd = jnp.concatenate([kp_ref[0, :, gs], kc_ref[0, :, gs]], axis=0)
        vd = jnp.concatenate([vp_ref[0, :, gs], vc_ref[0, :, gs]], axis=0)
        for pr in range(SWA_HEADS // SWA_KV_HEADS // 2):
            blk = g * 4 + pr
            q2 = q_ref[0, :, blk * LANES:(blk + 1) * LANES]
            outs = []
            for sub in range(2):
                qm = jnp.where(lo if sub == 0 else jnp.logical_not(lo), q2, jnp.zeros_like(q2))
                s = lax.dot_general(qm, kd, (((1,), (1,)), ((), ())), preferred_element_type=F32) * scale
                s = jnp.where(keep, s, NEG_INF)
                sink = sink_ref[2 * blk + sub]
                m = jnp.maximum(jnp.max(s, axis=-1, keepdims=True), sink)
                e = jnp.exp(s - m)
                denom = jnp.sum(e, axis=-1, keepdims=True) + jnp.exp(sink - m)
                p = (e / denom).astype(BF16)
                outs.append(jnp.dot(p, vd, preferred_element_type=F32))
            o_ref[0, :, blk * LANES:(blk + 1) * LANES] = jnp.where(lo, outs[0], outs[1]).astype(BF16)


def _swa_attention(q, kdup, vdup, sinks):
    b, s, _ = q.shape
    w = SWA_WINDOW
    kern = functools.partial(_swa_kernel, scale=SWA_HEAD_DIM ** -0.5)
    cur = lambda bi, n, sk: (bi, n, 0)
    prev = lambda bi, n, sk: (bi, jnp.maximum(n - 1, 0), 0)
    return pl.pallas_call(
        kern,
        out_shape=jax.ShapeDtypeStruct((b, s, 1024), BF16),
        grid_spec=pltpu.PrefetchScalarGridSpec(
            num_scalar_prefetch=1,
            grid=(b, s // w),
            in_specs=[pl.BlockSpec((1, w, 1024), cur),
                      pl.BlockSpec((1, w, 256), prev), pl.BlockSpec((1, w, 256), cur),
                      pl.BlockSpec((1, w, 256), prev), pl.BlockSpec((1, w, 256), cur)],
            out_specs=pl.BlockSpec((1, w, 1024), cur)),
        compiler_params=_cparams("parallel", "parallel"),
        name="swa_attn",
    )(sinks, q, kdup, kdup, vdup, vdup)


def _out_kernel(*refs, n_in, has_bias):
    acts = refs[:n_in]
    ws = refs[n_in:2 * n_in]
    pos = 2 * n_in
    bias_ref = refs[pos] if has_bias else None
    pos += 1 if has_bias else 0
    x_ref, g1_ref, sh_ref, sc_ref, gn_ref, rwh_ref, rwl_ref, rb_ref = refs[pos:pos + 8]
    x1_ref, h2_ref, gate_ref, idx_ref = refs[pos + 8:pos + 12]

    mix = jnp.dot(acts[0][0], ws[0][...], preferred_element_type=F32)
    for a_ref, w_ref in zip(acts[1:], ws[1:]):
        mix = mix + jnp.dot(a_ref[0], w_ref[...], preferred_element_type=F32)
    if has_bias:
        mix = mix + bias_ref[...]
    x1 = x_ref[0] + g1_ref[0] * mix
    x1_ref[0] = x1
    tt = x1.shape[0]
    h2 = _rms(x1, gn_ref[...]) * (1.0 + sc_ref[0]) + sh_ref[0]
    for s in range(SLAB):
        h2_ref[pl.ds(s, tt, stride=SLAB), :] = h2[:, s * LANES:(s + 1) * LANES]

    h_hi = h2.astype(BF16)
    h_lo = (h2 - h_hi.astype(F32)).astype(BF16)
    logits = (jnp.dot(h_hi, rwh_ref[...], preferred_element_type=F32)
              + jnp.dot(h_lo, rwh_ref[...], preferred_element_type=F32)
              + jnp.dot(h_hi, rwl_ref[...], preferred_element_type=F32)) + rb_ref[...]
    lane = _lane_iota(tt)
    lane_f = lane.astype(F32)
    cur = jnp.where(lane < N_EXPERTS, logits, -jnp.inf)
    vals, idxs = [], []
    for _ in range(TOP_K):
        m = jnp.max(cur, axis=-1, keepdims=True)
        first = jnp.min(jnp.where(cur == m, lane_f, float(LANES)), axis=-1, keepdims=True)
        vals.append(m)
        idxs.append(first)
        cur = jnp.where(lane_f == first, -jnp.inf, cur)
    es = [jnp.exp(v - vals[0]) for v in vals]
    denom = es[0] + es[1] + es[2] + es[3]
    gate = jnp.zeros((tt, LANES), F32)
    sel = jnp.zeros((tt, LANES), F32)
    for k in range(TOP_K):
        gate = jnp.where(lane == k, es[k] / denom, gate)
        sel = jnp.where(lane == k, idxs[k], sel)
    gate_ref[0] = gate
    idx_ref[0] = sel.astype(jnp.int32)


def _out_router(acts, ws, bias, x, modr, layer, gn, rw_hi, rw_lo, rb):
    b, s, d = x.shape
    tt = TOKEN_TILE
    nt = s // tt
    row = lambda bi, i: (bi, i, 0)
    const = lambda bi, i: (0, 0)
    mod_idx = lambda which: (lambda bi, i: ((layer * b + bi) * 6 + which, 0, 0))
    kern = functools.partial(_out_kernel, n_in=len(acts), has_bias=bias is not None)
    in_specs = [pl.BlockSpec((1, tt, a.shape[2]), row) for a in acts] + \
               [pl.BlockSpec(w.shape, const) for w in ws]
    args = list(acts) + list(ws)
    if bias is not None:
        in_specs.append(pl.BlockSpec((1, d), const))
        args.append(bias)
    in_specs += [pl.BlockSpec((1, tt, d), row),
                 pl.BlockSpec((1, 1, d), mod_idx(2)), pl.BlockSpec((1, 1, d), mod_idx(3)),
                 pl.BlockSpec((1, 1, d), mod_idx(4)),
                 pl.BlockSpec((1, d), const), pl.BlockSpec((d, LANES), const), pl.BlockSpec((d, LANES), const),
                 pl.BlockSpec((1, LANES), const)]
    args += [x, modr, modr, modr, gn, rw_hi, rw_lo, rb]
    return pl.pallas_call(
        kern,
        out_shape=[jax.ShapeDtypeStruct((b, s, d), F32),
                   jax.ShapeDtypeStruct((b * s * SLAB, LANES), F32),
                   jax.ShapeDtypeStruct((b, s, LANES), F32),
                   jax.ShapeDtypeStruct((b, s, LANES), jnp.int32)],
        grid=(b, nt),
        in_specs=in_specs,
        out_specs=[pl.BlockSpec((1, tt, d), row),
                   pl.BlockSpec((tt * SLAB, LANES), lambda bi, i: (bi * nt + i, 0)),
                   pl.BlockSpec((1, tt, LANES), row), pl.BlockSpec((1, tt, LANES), row)],
        compiler_params=_cparams("parallel", "parallel"),
        name="out_router",
    )(*args)


def _route(top_i, n_tok):
    tm = MOE_TILE
    m = n_tok * TOP_K
    n_tiles = m // tm + N_EXPERTS
    e_flat = top_i.reshape(m)
    order = jnp.argsort(e_flat, stable=True).astype(jnp.int32)
    counts = jnp.sum((e_flat[:, None] == jnp.arange(N_EXPERTS, dtype=jnp.int32)[None, :]).astype(jnp.int32), axis=0)
    start = jnp.cumsum(counts) - counts
    ntile_e = (counts + tm - 1) // tm
    tile_end = jnp.cumsum(ntile_e)
    tile_start = tile_end - ntile_e
    n_used = tile_end[-1]
    t = jnp.arange(n_tiles, dtype=jnp.int32)
    tile_e = jnp.minimum(jnp.sum((t[:, None] >= tile_end[None, :]).astype(jnp.int32), axis=1), N_EXPERTS - 1)
    r = jnp.arange(tm, dtype=jnp.int32)
    rank = (t - tile_start[tile_e])[:, None] * tm + r[None, :]
    cnt = counts[tile_e][:, None]
    valid = (t[:, None] < n_used) & (rank < cnt)
    src = jnp.clip(start[tile_e][:, None] + rank, 0, m - 1)
    flat = order[src]
    slot_tok = jnp.where(valid, flat // TOP_K, 0).astype(jnp.int32)
    slot_dst = jnp.where(valid, (flat % TOP_K) * n_tok + flat // TOP_K, 0).astype(jnp.int32)
    tile_nv = jnp.sum(valid.astype(jnp.int32), axis=1)
    return (tile_e, n_used.reshape(1).astype(jnp.int32), tile_nv,
            slot_tok.reshape(n_tiles, 1, tm), slot_dst.reshape(n_tiles, 1, tm))


def _ffn_kernel(te_ref, nu_ref, nv_ref, tok_ref, tokn_ref, dst_ref, h_hbm, wgu_ref, bgu_ref, wd_ref, bd_ref,
                y_hbm, xbuf, obuf, gsem, ssem):
    t = pl.program_id(0)
    nu = nu_ref[0]
    tm = tok_ref.shape[2]
    slot = lax.rem(t, 2)

    def gather_copy(idx_ref, r, sl):
        tok = idx_ref[0, 0, r]
        return pltpu.make_async_copy(h_hbm.at[pl.ds(pl.multiple_of(tok * SLAB, SLAB), SLAB), :],
                                     xbuf.at[sl, pl.ds(pl.multiple_of(r * SLAB, SLAB), SLAB), :],
                                     gsem.at[sl])

    def scatter_copy(r, sl):
        dst = dst_ref[0, 0, r]
        return pltpu.make_async_copy(obuf.at[sl, pl.ds(pl.multiple_of(r * SLAB, SLAB), SLAB), :],
                                     y_hbm.at[pl.ds(pl.multiple_of(dst * SLAB, SLAB), SLAB), :],
                                     ssem.at[sl])

    def issue_gather(idx_ref, sl):
        def body(r, carry):
            gather_copy(idx_ref, r, sl).start()
            return carry
        lax.fori_loop(0, tm, body, 0, unroll=8)

    def wait_rows(buf, sem, sl, rows):
        view = buf.at[sl, pl.ds(0, rows * SLAB), :]
        pltpu.make_async_copy(view, view, sem.at[sl]).wait()

    @pl.when(t < nu)
    def _():
        @pl.when(t == 0)
        def _():
            issue_gather(tok_ref, 0)

        @pl.when(t + 1 < nu)
        def _():
            issue_gather(tokn_ref, 1 - slot)

        wait_rows(xbuf, gsem, slot, tm)
        x = jnp.concatenate([xbuf[slot, pl.ds(s, tm, stride=SLAB), :].astype(BF16) for s in range(SLAB)],
                            axis=-1)
        gu = jnp.dot(x, wgu_ref[0], preferred_element_type=F32) + bgu_ref[0]
        de = gu.shape[1] // 2
        glu = jnp.minimum(gu[:, :de], SWIGLU_LIMIT)
        lin = jnp.clip(gu[:, de:], -SWIGLU_LIMIT, SWIGLU_LIMIT)
        act = glu * jax.nn.sigmoid(SWIGLU_ALPHA * glu) * (lin + 1.0)
        out = jnp.dot(act.astype(BF16), wd_ref[0], preferred_element_type=F32) + bd_ref[0]

        @pl.when(t >= 2)
        def _():
            wait_rows(obuf, ssem, slot, nv_ref[t - 2])

        for s in range(SLAB):
            obuf[slot, pl.ds(s, tm, stride=SLAB), :] = out[:, s * LANES:(s + 1) * LANES]

        nv = nv_ref[t]
        groups = lax.shift_right_logical(nv, 3)

        def sbody8(g, carry):
            for u in range(8):
                scatter_copy(g * 8 + u, slot).start()
            return carry
        lax.fori_loop(0, groups, sbody8, 0)

        def sbody(r, carry):
            scatter_copy(r, slot).start()
            return carry
        lax.fori_loop(groups * 8, nv, sbody, 0)

        @pl.when(t == nu - 1)
        def _():
            wait_rows(obuf, ssem, slot, nv)

            @pl.when(t >= 1)
            def _():
                wait_rows(obuf, ssem, 1 - slot, nv_ref[jnp.maximum(t - 1, 0)])


def _moe_ffn(h2_slab, tile_e, n_used, tile_nv, slot_tok, slot_dst, wgu, bgu, wd, bd):
    n_tiles, _, tm = slot_tok.shape
    n_exp, d, de2 = wgu.shape
    rows_out = h2_slab.shape[0] * TOP_K
    cur = lambda t, te, nu, nv: (t, 0, 0)
    nxt = lambda t, te, nu, nv: (jnp.minimum(t + 1, n_tiles - 1), 0, 0)
    w_idx = lambda t, te, nu, nv: (te[jnp.minimum(t, nu[0] - 1)], 0, 0)
    smem = functools.partial(pl.BlockSpec, memory_space=pltpu.SMEM)
    return pl.pallas_call(
        _ffn_kernel,
        out_shape=jax.ShapeDtypeStruct((rows_out, LANES), F32),
        grid_spec=pltpu.PrefetchScalarGridSpec(
            num_scalar_prefetch=3,
            grid=(n_tiles,),
            in_specs=[smem((1, 1, tm), cur), smem((1, 1, tm), nxt), smem((1, 1, tm), cur),
                      pl.BlockSpec(memory_space=pl.ANY),
                      pl.BlockSpec((1, d, de2), w_idx), pl.BlockSpec((1, 1, de2), w_idx),
                      pl.BlockSpec((1, de2 // 2, d), w_idx), pl.BlockSpec((1, 1, d), w_idx)],
            out_specs=pl.BlockSpec(memory_space=pl.ANY),
            scratch_shapes=[pltpu.VMEM((2, tm * SLAB, LANES), F32), pltpu.VMEM((2, tm * SLAB, LANES), F32),
                            pltpu.SemaphoreType.DMA((2,)), pltpu.SemaphoreType.DMA((2,))]),
        compiler_params=_cparams("arbitrary"),
        name="moe_ffn",
    )(tile_e, n_used, tile_nv, slot_tok, slot_tok, slot_dst, h2_slab, wgu, bgu.reshape(n_exp, 1, de2), wd,
      bd.reshape(n_exp, 1, d))


def _combine_kernel(y0_ref, y1_ref, y2_ref, y3_ref, x1_ref, g2_ref, gate_ref, o_ref):
    tc = x1_ref.shape[1]
    gate = gate_ref[0]
    y_refs = (y0_ref, y1_ref, y2_ref, y3_ref)
    gk = [gate[:, k:k + 1] for k in range(TOP_K)]
    x1 = x1_ref[0]
    g2 = g2_ref[0]
    for s in range(SLAB):
        acc = gk[0] * y_refs[0][pl.ds(s, tc, stride=SLAB), :]
        for k in range(1, TOP_K):
            acc = acc + gk[k] * y_refs[k][pl.ds(s, tc, stride=SLAB), :]
        sl = slice(s * LANES, (s + 1) * LANES)
        o_ref[0, :, sl] = x1[:, sl] + g2[:, sl] * acc


def _combine(y_slab, x1, modr, layer, gates):
    b, s, d = x1.shape
    tc = COMBINE_TILE
    nt = s // tc
    row = lambda bi, i: (bi, i, 0)
    plane = lambda k: pl.BlockSpec((tc * SLAB, LANES), lambda bi, i: (k * b * nt + bi * nt + i, 0))
    return pl.pallas_call(
        _combine_kernel,
        out_shape=jax.ShapeDtypeStruct((b, s, d), F32),
        grid=(b, nt),
        in_specs=[plane(k) for k in range(TOP_K)] +
                 [pl.BlockSpec((1, tc, d), row),
                  pl.BlockSpec((1, 1, d), lambda bi, i: ((layer * b + bi) * 6 + 5, 0, 0)),
                  pl.BlockSpec((1, tc, LANES), row)],
        out_specs=pl.BlockSpec((1, tc, d), row),
        compiler_params=_cparams("parallel", "parallel"),
        name="moe_combine",
    )(y_slab, y_slab, y_slab, y_slab, x1, modr, gates)


def _pad_cols(w, groups, width, to):
    k = w.shape[0]
    w3 = w.reshape(k, groups, width)
    return jnp.pad(w3, ((0, 0), (0, 0), (0, to - width))).reshape(k, groups * to)


def _rope_tables(positions):
    pos = positions.astype(F32)[:, :, None]
    def cs(half):
        inv = ROPE_THETA ** (-jnp.arange(half, dtype=F32) / half)
        ang = pos * inv
        return jnp.cos(ang), jnp.sin(ang)
    c16, s16 = cs(MLA_ROPE_DIM // 2)
    b, s = positions.shape
    ones, zeros = jnp.ones((b, s, 64), F32), jnp.zeros((b, s, 32), F32)
    cm = jnp.concatenate([ones, c16, c16, zeros], axis=-1)
    sm = jnp.concatenate([jnp.zeros((b, s, 64), F32), -s16, s16, zeros], axis=-1)
    c32, s32 = cs(RET_DIM // 2)
    c64 = jnp.concatenate([c32, c32, c32, c32], axis=-1)
    s64 = jnp.concatenate([-s32, s32, -s32, s32], axis=-1)
    return cm, sm, c64, s64


def _router_split(router_w, router_b):
    d = router_w.shape[0]
    w = jnp.pad(router_w, ((0, 0), (0, LANES - N_EXPERTS)))
    hi = w.astype(BF16)
    lo = (w - hi.astype(F32)).astype(BF16)
    rb = jnp.pad(router_b, (0, LANES - N_EXPERTS)).reshape(1, LANES)
    return hi, lo, rb


def _moe_layer(x1, h2_slab, gates, top_i, modr, layer, w_gu, b_gu, w_down, b_down):
    b, s, d = x1.shape
    tile_e, n_used, tile_nv, slot_tok, slot_dst = _route(top_i[..., :TOP_K], b * s)
    y_slab = _moe_ffn(h2_slab, tile_e, n_used, tile_nv, slot_tok, slot_dst,
                      w_gu.astype(BF16), b_gu, w_down.astype(BF16), b_down)
    return _combine(y_slab, x1, modr, layer, gates)


def kernel(x, c, positions, ada_w, ada_b, norm1_g, norm2_g, hyb_w_in, mla_cq_norm_g, mla_ckv_norm_g, mla_w_uq, mla_w_ukv, mla_q_head_g, mla_k_head_g, ret_norm_g, hyb_w_out, swa_w_qkv, swa_b_qkv, swa_q_head_g, swa_k_head_g, swa_sinks, swa_w_out, swa_b_out, router_w, router_b, exp_w_gu, exp_b_gu, exp_w_down, exp_b_down):
    b, s, d = x.shape
    depth = ada_w.shape[0]
    mod = _ada_mod(c, ada_w, ada_b)
    modr = mod.reshape(depth * b * 6, 1, d)
    cm, sm, c64, s64 = _rope_tables(positions)

    for layer in range(depth):
        j = layer // 2
        g1 = norm1_g[layer].reshape(1, d)
        g2 = norm2_g[layer].reshape(1, d)
        if layer % 2 == 0:
            w_in = hyb_w_in[j]
            zc = lambda n: jnp.zeros((d, n), F32)
            win_p = jnp.concatenate([w_in[:, :640], zc(64), w_in[:, 640:672], zc(32), w_in[:, 672:]],
                                    axis=1).astype(BF16)
            wuq_p = _pad_cols(mla_w_uq[j], MLA_HEADS, MLA_QK_DIM, LANES).astype(BF16)
            wukv3 = mla_w_ukv[j].reshape(MLA_KV_RANK, MLA_HEADS, MLA_NOPE_DIM + MLA_V_DIM)
            wk = jnp.pad(wukv3[:, :, :MLA_NOPE_DIM], ((0, 0), (0, 0), (0, LANES - MLA_NOPE_DIM)))
            wukv_p = jnp.concatenate([wk.reshape(MLA_KV_RANK, MLA_HEADS * LANES),
                                      wukv3[:, :, MLA_NOPE_DIM:].reshape(MLA_KV_RANK, MLA_HEADS * MLA_V_DIM)],
                                     axis=1).astype(BF16)
            z32, z64 = jnp.zeros((32,), F32), jnp.zeros((64,), F32)
            gq = jnp.concatenate([mla_q_head_g[j], z32]).reshape(1, LANES)
            gkn = jnp.concatenate([mla_k_head_g[j][:MLA_NOPE_DIM], z64]).reshape(1, LANES)
            gkr = jnp.concatenate([z64, mla_k_head_g[j][MLA_NOPE_DIM:], z32]).reshape(1, LANES)
            q, k, v, rq, rk, rv, rg = _in0(x, modr, layer, g1, win_p, mla_cq_norm_g[j].reshape(1, -1),
                                           mla_ckv_norm_g[j].reshape(1, -1), wuq_p, wukv_p, gq, gkn, gkr,
                                           cm, sm, c64, s64)
            attn = _mla_attention(q, k, v)
            ret = _retention(rq, rk, rv, rg, ret_norm_g[j])
            w_out = hyb_w_out[j].astype(BF16)
            acts, ws, bias = [attn, ret], [w_out[:512], w_out[512:]], None
        else:
            wq = swa_w_qkv[j]
            bq = swa_b_qkv[j]
            nq = SWA_HEADS * SWA_HEAD_DIM
            hd = SWA_HEAD_DIM
            dup = lambda t: jnp.concatenate([t[..., 0:hd], t[..., 0:hd], t[..., hd:2 * hd], t[..., hd:2 * hd]], axis=-1)
            w_p = jnp.concatenate([wq[:, :nq], dup(wq[:, nq:nq + 2 * hd]), dup(wq[:, nq + 2 * hd:])], axis=1).astype(BF16)
            b_p = jnp.concatenate([bq[:nq], dup(bq[nq:nq + 2 * hd]), dup(bq[nq + 2 * hd:])]).reshape(1, -1)
            gq = jnp.concatenate([swa_q_head_g[j], swa_q_head_g[j]]).reshape(1, LANES)
            gk = jnp.concatenate([swa_k_head_g[j], swa_k_head_g[j]]).reshape(1, LANES)
            q, kdup, vdup = _in1(x, modr, layer, g1, w_p, b_p, gq, gk, c64, s64)
            o = _swa_attention(q, kdup, vdup, swa_sinks[j])
            acts, ws, bias = [o], [swa_w_out[j].astype(BF16)], swa_b_out[j].reshape(1, d)
        rw_hi, rw_lo, rb = _router_split(router_w[layer], router_b[layer])
        x1, h2_slab, gates, top_i = _out_router(acts, ws, bias, x, modr, layer, g2, rw_hi, rw_lo, rb)
        x = _moe_layer(x1, h2_slab, gates, top_i, modr, layer, exp_w_gu[layer], exp_b_gu[layer],
                       exp_w_down[layer], exp_b_down[layer])
    return x
```
